```python
import math
import jax, jax.numpy as jnp
from jax import lax
import numpy as np

D_MODEL = 1024
BATCH = 8
SEQ = 2048
DEPTH = 4
DEC_BATCH = 128
DEC_SEQ = 1
PAST_LEN = 16384
PAGE_SIZE = 128

D_MIX = D_MODEL
HEAD_DIM = 64
SSD_WIDTH = D_MIX // 2
SSD_HEADS = SSD_WIDTH // HEAD_DIM
SSD_GROUPS = 2
SSD_STATE = 128
SSD_CONV = 4
SSD_CONV_DIM = SSD_WIDTH + 2 * SSD_GROUPS * SSD_STATE
GLA_WIDTH = D_MIX // 4
GLA_HEADS = GLA_WIDTH // HEAD_DIM
GLA_DK = HEAD_DIM // 2
GLA_QK = GLA_HEADS * GLA_DK
GLA_RANK = 16
GLA_TAU = 16.0
RET_WIDTH = D_MIX - SSD_WIDTH - GLA_WIDTH
RET_HEADS = RET_WIDTH // HEAD_DIM
ROPE_BASE = 10000.0
MEM_LEN = 256
XATTN_HEADS = 4
XATTN_HEAD_DIM = D_MODEL // XATTN_HEADS
FFN_HIDDEN = 2816
CHUNK = 128
EPS = 1e-6
IN_SIZES = (SSD_WIDTH, SSD_CONV_DIM, SSD_HEADS, GLA_QK, GLA_QK, GLA_WIDTH, GLA_WIDTH, GLA_RANK, RET_WIDTH, RET_WIDTH, RET_WIDTH, RET_WIDTH)
D_IN = sum(IN_SIZES)

kernel_name = 'hybrid_ssd_gla_ret_macaron_step'


def rmsnorm(x, w):
    xf = x.astype(jnp.float32)
    y = xf * lax.rsqrt(jnp.mean(xf * xf, axis=-1, keepdims=True) + EPS)
    return (y * w.astype(jnp.float32)).astype(x.dtype)


def split_cols(t, sizes):
    offs, acc = [], 0
    for s in sizes[:-1]:
        acc += s
        offs.append(acc)
    return jnp.split(t, offs, axis=-1)


def swiglu(x, w_up, w_down):
    gate, up = jnp.split(x @ w_up, 2, axis=-1)
    return (jax.nn.silu(gate) * up) @ w_down


def causal_dwconv(u, buf, w, b):
    full = jnp.concatenate([buf.astype(u.dtype), u], axis=1)
    out = lax.conv_general_dilated(full, w[:, None, :].astype(u.dtype), window_strides=(1,), padding='VALID',
                                   dimension_numbers=('NWC', 'WIO', 'NWC'), feature_group_count=u.shape[-1])
    return jax.nn.silu(out + b), full[:, full.shape[1] - (SSD_CONV - 1):]


def _to_chunks(t, n, c):
    return jnp.moveaxis(t.reshape((t.shape[0], n, c) + t.shape[2:]), 1, 0)


def _from_chunks(t):
    n, b, c = t.shape[:3]
    return jnp.moveaxis(t, 0, 1).reshape((b, n * c) + t.shape[3:])


def chunk_scan_scalar(q, k, v, log_a, s0):
    L = q.shape[1]
    c = math.gcd(L, CHUNK)
    n = L // c
    mask = jnp.tril(jnp.ones((c, c), dtype=bool))[None, :, :, None]

    def step(s, inp):
        qc, kc, vc, ac = (t.astype(jnp.float32) for t in inp)
        b = jnp.cumsum(ac, axis=1)
        decay = jnp.exp(jnp.where(mask, b[:, :, None, :] - b[:, None, :, :], -jnp.inf))
        scores = jnp.einsum('bthk,bshk->btsh', qc, kc) * decay
        o = jnp.einsum('btsh,bshv->bthv', scores, vc) + jnp.einsum('bthk,bhkv->bthv', qc, s) * jnp.exp(b)[..., None]
        tail = jnp.exp(b[:, -1:] - b)
        s = s * jnp.exp(b[:, -1])[:, :, None, None] + jnp.einsum('bshk,bshv->bhkv', kc * tail[..., None], vc)
        return s, o

    s_fin, o = lax.scan(step, s0.astype(jnp.float32), tuple(_to_chunks(t, n, c) for t in (q, k, v, log_a)))
    return _from_chunks(o).astype(v.dtype), s_fin.astype(s0.dtype)


def chunk_scan_vector(q, k, v, log_a, s0):
    L = q.shape[1]
    c = math.gcd(L, CHUNK)
    n = L // c
    mask = jnp.tril(jnp.ones((c, c), dtype=bool))[None, :, :, None, None]

    def step(s, inp):
        qc, kc, vc, ac = (t.astype(jnp.float32) for t in inp)
        b = jnp.cumsum(ac, axis=1)
        decay = jnp.exp(jnp.where(mask, b[:, :, None] - b[:, None], -jnp.inf))
        scores = jnp.einsum('bthk,bshk,btshk->btsh', qc, kc, decay)
        o = jnp.einsum('btsh,bshv->bthv', scores, vc) + jnp.einsum('bthk,bhkv->bthv', qc * jnp.exp(b), s)
        s = s * jnp.exp(b[:, -1])[..., None] + jnp.einsum('bshk,bshv->bhkv', kc * jnp.exp(b[:, -1:] - b), vc)
        return s, o

    s_fin, o = lax.scan(step, s0.astype(jnp.float32), tuple(_to_chunks(t, n, c) for t in (q, k, v, log_a)))
    return _from_chunks(o).astype(v.dtype), s_fin.astype(s0.dtype)


def rotary(x, pos):
    half = x.shape[-1] // 2
    inv = ROPE_BASE ** (-jnp.arange(half, dtype=jnp.float32) / half)
    ang = pos.astype(jnp.float32)[:, None] * inv[None, :]
    cos = jnp.cos(ang)[None, :, None, :]
    sin = jnp.sin(ang)[None, :, None, :]
    xf = x.astype(jnp.float32)
    x1, x2 = xf[..., :half], xf[..., half:]
    return jnp.concatenate([x1 * cos - x2 * sin, x1 * sin + x2 * cos], axis=-1)


def head_rmsnorm(o, w):
    of = o.astype(jnp.float32)
    y = of * lax.rsqrt(jnp.mean(of * of, axis=-1, keepdims=True) + EPS)
    return y.reshape(o.shape[:2] + (-1,)) * w.astype(jnp.float32)


def head_groupnorm(o, w):
    of = o.astype(jnp.float32)
    mu = jnp.mean(of, axis=-1, keepdims=True)
    d = of - mu
    y = d * lax.rsqrt(jnp.mean(d * d, axis=-1, keepdims=True) + EPS)
    return y.reshape(o.shape[:2] + (-1,)) * w.astype(jnp.float32)


def token_mix(h, pos, state, w_in, conv_w, conv_b, dt_bias, a_log, d_skip, ssd_norm_w,
              gla_w_gate, gla_b_gate, gla_norm_w, ret_norm_w, w_out):
    s_ssd, conv_buf, s_gla, s_ret = state
    bsz, L, _ = h.shape
    z, xbc, dt_raw, gq, gk, gv, gg, glr, rq, rk, rv, rg = split_cols(h @ w_in, IN_SIZES)
    xbc, conv_new = causal_dwconv(xbc, conv_buf, conv_w, conv_b)
    xs, bm, cm = split_cols(xbc, (SSD_WIDTH, SSD_GROUPS * SSD_STATE, SSD_GROUPS * SSD_STATE))
    xs = xs.reshape(bsz, L, SSD_HEADS, HEAD_DIM)
    rep = SSD_HEADS // SSD_GROUPS
    bm = jnp.repeat(bm.reshape(bsz, L, SSD_GROUPS, SSD_STATE), rep, axis=2)
    cm = jnp.repeat(cm.reshape(bsz, L, SSD_GROUPS, SSD_STATE), rep, axis=2)
    dt = jax.nn.softplus(dt_raw.astype(jnp.float32) + dt_bias.astype(jnp.float32))
    a = -jnp.exp(a_log.astype(jnp.float32))
    y_ssd, s_ssd_new = chunk_scan_scalar(cm, bm * dt[..., None], xs, dt * a, s_ssd)
    y_ssd = (y_ssd + xs * d_skip[:, None]).reshape(bsz, L, SSD_WIDTH)
    g = (y_ssd.astype(jnp.float32) * jax.nn.silu(z.astype(jnp.float32))).reshape(bsz, L, SSD_GROUPS, -1)
    y_ssd = (g * lax.rsqrt(jnp.mean(g * g, axis=-1, keepdims=True) + EPS)).reshape(bsz, L, SSD_WIDTH) * ssd_norm_w
    gq = gq.reshape(bsz, L, GLA_HEADS, GLA_DK) * GLA_DK ** -0.5
    gk = gk.reshape(bsz, L, GLA_HEADS, GLA_DK)
    gv = gv.reshape(bsz, L, GLA_HEADS, HEAD_DIM)
    log_alpha = jax.nn.log_sigmoid((glr @ gla_w_gate + gla_b_gate).astype(jnp.float32)) / GLA_TAU
    log_alpha = log_alpha.reshape(bsz, L, GLA_HEADS, GLA_DK)
    o_gla, s_gla_new = chunk_scan_vector(gq, gk, gv, log_alpha, s_gla)
    y_gla = head_rmsnorm(o_gla, gla_norm_w) * jax.nn.silu(gg.astype(jnp.float32))
    rq = rotary(rq.reshape(bsz, L, RET_HEADS, HEAD_DIM), pos)
    rk = rotary(rk.reshape(bsz, L, RET_HEADS, HEAD_DIM), pos) * HEAD_DIM ** -0.5
    rv = rv.reshape(bsz, L, RET_HEADS, HEAD_DIM)
    log_gamma = jnp.log1p(-jnp.exp2(-5.0 - jnp.arange(RET_HEADS, dtype=jnp.float32)))
    o_ret, s_ret_new = chunk_scan_scalar(rq, rk, rv, jnp.broadcast_to(log_gamma, (bsz, L, RET_HEADS)), s_ret)
    y_ret = head_groupnorm(o_ret, ret_norm_w) * jax.nn.silu(rg.astype(jnp.float32))
    y = jnp.concatenate([y_ssd, y_gla, y_ret], axis=-1).astype(h.dtype) @ w_out
    return y, (s_ssd_new, conv_new, s_gla_new, s_ret_new)


def cross_attend(h, mem_k, mem_v, w_q, w_o):
    bsz, L, _ = h.shape
    q = (h @ w_q).reshape(bsz, L, XATTN_HEADS, XATTN_HEAD_DIM)
    s = jnp.einsum('blhd,bmhd->bhlm', q, mem_k).astype(jnp.float32) * XATTN_HEAD_DIM ** -0.5
    p = jax.nn.softmax(s, axis=-1)
    o = jnp.einsum('bhlm,bmhd->blhd', p.astype(mem_v.dtype), mem_v).reshape(bsz, L, D_MODEL)
    return o @ w_o


def decoder_layer(x, pos, mem_k, mem_v, state, lw):
    (n1, f1u, f1d, n2, w_in, cw, cb, dtb, alog, dsk, ssdw, gw, gb, glaw, retw, wo,
     n3, xq, xo, n4, f2u, f2d) = lw
    x = x + 0.5 * swiglu(rmsnorm(x, n1), f1u, f1d)
    mix, new_state = token_mix(rmsnorm(x, n2), pos, state, w_in, cw, cb, dtb, alog, dsk, ssdw, gw, gb, glaw, retw, wo)
    x = x + mix
    x = x + cross_attend(rmsnorm(x, n3), mem_k, mem_v, xq, xo)
    x = x + 0.5 * swiglu(rmsnorm(x, n4), f2u, f2d)
    return x, new_state


def setup_inputs(seed: int = 0) -> dict:
    key = jax.random.key(seed)
    k = jax.random.split(key, 40)

    def nrm(i, shape, scale=1.0):
        return jax.random.normal(k[i], shape, jnp.float32) * scale

    def gain(i, shape):
        return 1.0 + nrm(i, shape, 0.02)

    dt0 = jnp.exp(jax.random.uniform(k[33], (DEPTH, SSD_HEADS), jnp.float32, minval=math.log(1e-3), maxval=math.log(1e-1)))
    dt_bias = dt0 + jnp.log(-jnp.expm1(-dt0))
    a_log = jnp.log(jax.random.uniform(k[34], (DEPTH, SSD_HEADS), jnp.float32, minval=1.0, maxval=16.0))
    return {
        'x_prompt': nrm(0, (BATCH, SEQ, D_MODEL)),
        'x_sample': nrm(1, (DEC_BATCH, DEC_SEQ, D_MODEL)),
        'mem_prompt': nrm(2, (BATCH, MEM_LEN, D_MODEL)),
        'cache_mem_k': nrm(3, (DEPTH, DEC_BATCH, MEM_LEN, XATTN_HEADS, XATTN_HEAD_DIM)),
        'cache_mem_v': nrm(4, (DEPTH, DEC_BATCH, MEM_LEN, XATTN_HEADS, XATTN_HEAD_DIM)),
        'state_ssd': nrm(5, (DEPTH, DEC_BATCH, SSD_HEADS, SSD_STATE, HEAD_DIM), 0.1),
        'state_ssd_conv': nrm(6, (DEPTH, DEC_BATCH, SSD_CONV - 1, SSD_CONV_DIM)),
        'state_gla': nrm(7, (DEPTH, DEC_BATCH, GLA_HEADS, GLA_DK, HEAD_DIM), 0.1),
        'state_ret': nrm(8, (DEPTH, DEC_BATCH, RET_HEADS, HEAD_DIM, HEAD_DIM), 0.1),
        'norm_ffn1': gain(9, (DEPTH, D_MODEL)),
        'ffn1_w_up': nrm(10, (DEPTH, D_MODEL, 2 * FFN_HIDDEN), D_MODEL ** -0.5),
        'ffn1_w_down': nrm(11, (DEPTH, FFN_HIDDEN, D_MODEL), FFN_HIDDEN ** -0.5),
        'norm_mix': gain(12, (DEPTH, D_MODEL)),
        'w_in': nrm(13, (DEPTH, D_MODEL, D_IN), D_MODEL ** -0.5),
        'conv_w': nrm(14, (DEPTH, SSD_CONV, SSD_CONV_DIM), SSD_CONV ** -0.5),
        'conv_b': nrm(15, (DEPTH, SSD_CONV_DIM), 0.02),
        'dt_bias': dt_bias,
        'a_log': a_log,
        'd_skip': gain(16, (DEPTH, SSD_HEADS)),
        'ssd_norm_w': gain(17, (DEPTH, SSD_WIDTH)),
        'gla_w_gate': nrm(18, (DEPTH, GLA_RANK, GLA_QK), GLA_RANK ** -0.5),
        'gla_b_gate': nrm(19, (DEPTH, GLA_QK), 0.02),
        'gla_norm_w': gain(20, (DEPTH, GLA_WIDTH)),
        'ret_norm_w': gain(21, (DEPTH, RET_WIDTH)),
        'w_out': nrm(22, (DEPTH, D_MIX, D_MODEL), D_MIX ** -0.5),
        'norm_xattn': gain(23, (DEPTH, D_MODEL)),
        'xattn_w_q': nrm(24, (DEPTH, D_MODEL, D_MODEL), D_MODEL ** -0.5),
        'xattn_w_k': nrm(25, (DEPTH, D_MODEL, D_MODEL), D_MODEL ** -0.5),
        'xattn_w_v': nrm(26, (DEPTH, D_MODEL, D_MODEL), D_MODEL ** -0.5),
        'xattn_w_o': nrm(27, (DEPTH, D_MODEL, D_MODEL), D_MODEL ** -0.5),
        'norm_ffn2': gain(28, (DEPTH, D_MODEL)),
        'ffn2_w_up': nrm(29, (DEPTH, D_MODEL, 2 * FFN_HIDDEN), D_MODEL ** -0.5),
        'ffn2_w_down': nrm(30, (DEPTH, FFN_HIDDEN, D_MODEL), FFN_HIDDEN ** -0.5),
        'final_norm_w': gain(31, (D_MODEL,)),
    }


def reference(x_prompt, x_sample, mem_prompt, cache_mem_k, cache_mem_v, state_ssd, state_ssd_conv, state_gla, state_ret,
              norm_ffn1, ffn1_w_up, ffn1_w_down, norm_mix, w_in, conv_w, conv_b, dt_bias, a_log, d_skip, ssd_norm_w,
              gla_w_gate, gla_b_gate, gla_norm_w, ret_norm_w, w_out, norm_xattn, xattn_w_q, xattn_w_k, xattn_w_v,
              xattn_w_o, norm_ffn2, ffn2_w_up, ffn2_w_down, final_norm_w):
    bp, lp = x_prompt.shape[0], x_prompt.shape[1]
    ls = x_sample.shape[1]
    pos_p = jnp.arange(lp, dtype=jnp.int32)
    pos_s = PAST_LEN + jnp.arange(ls, dtype=jnp.int32)
    dtp = x_prompt.dtype
    zero_state = (jnp.zeros((bp, SSD_HEADS, SSD_STATE, HEAD_DIM), dtp),
                  jnp.zeros((bp, SSD_CONV - 1, SSD_CONV_DIM), dtp),
                  jnp.zeros((bp, GLA_HEADS, GLA_DK, HEAD_DIM), dtp),
                  jnp.zeros((bp, RET_HEADS, HEAD_DIM, HEAD_DIM), dtp))
    xp, xs = x_prompt, x_sample
    ssd_p, conv_p, gla_p, ret_p, mk_p, mv_p = [], [], [], [], [], []
    ssd_s, conv_s, gla_s, ret_s = [], [], [], []
    for l in range(DEPTH):
        lw = (norm_ffn1[l], ffn1_w_up[l], ffn1_w_down[l], norm_mix[l], w_in[l], conv_w[l], conv_b[l], dt_bias[l],
              a_log[l], d_skip[l], ssd_norm_w[l], gla_w_gate[l], gla_b_gate[l], gla_norm_w[l], ret_norm_w[l], w_out[l],
              norm_xattn[l], xattn_w_q[l], xattn_w_o[l], norm_ffn2[l], ffn2_w_up[l], ffn2_w_down[l])
        mk = (mem_prompt @ xattn_w_k[l]).reshape(bp, MEM_LEN, XATTN_HEADS, XATTN_HEAD_DIM)
        mv = (mem_prompt @ xattn_w_v[l]).reshape(bp, MEM_LEN, XATTN_HEADS, XATTN_HEAD_DIM)
        xp, (s1, c1, g1, r1) = decoder_layer(xp, pos_p, mk, mv, zero_state, lw)
        xs, (s2, c2, g2, r2) = decoder_layer(xs, pos_s, cache_mem_k[l], cache_mem_v[l],
                                             (state_ssd[l], state_ssd_conv[l], state_gla[l], state_ret[l]), lw)
        ssd_p.append(s1); conv_p.append(c1); gla_p.append(g1); ret_p.append(r1); mk_p.append(mk); mv_p.append(mv)
        ssd_s.append(s2); conv_s.append(c2); gla_s.append(g2); ret_s.append(r2)
    y_prompt = rmsnorm(xp, final_norm_w)
    y_sample = rmsnorm(xs, final_norm_w)
    return (y_prompt, y_sample,
            jnp.stack(ssd_p), jnp.stack(conv_p), jnp.stack(gla_p), jnp.stack(ret_p), jnp.stack(mk_p), jnp.stack(mv_p),
            jnp.stack(ssd_s), jnp.stack(conv_s), jnp.stack(gla_s), jnp.stack(ret_s))
```

```python
import functools
import math

import jax
import jax.numpy as jnp
from jax import lax
from jax.experimental import pallas as pl
from jax.experimental.pallas import tpu as pltpu

F32 = jnp.float32
BF16 = jnp.bfloat16

D_MODEL = 1024
DEPTH = 4
PAST_LEN = 16384
HEAD_DIM = 64
SSD_WIDTH = 512
SSD_HEADS = 8
SSD_GROUPS = 2
SSD_STATE = 128
SSD_CONV = 4
SSD_CONV_DIM = SSD_WIDTH + 2 * SSD_GROUPS * SSD_STATE
GLA_WIDTH = 256
GLA_HEADS = 4
GLA_DK = 32
GLA_QK = GLA_HEADS * GLA_DK
GLA_RANK = 16
GLA_TAU = 16.0
RET_WIDTH = 256
RET_HEADS = 4
ROPE_BASE = 10000.0
MEM_LEN = 256
XATTN_HEADS = 4
XATTN_HEAD_DIM = D_MODEL // XATTN_HEADS
FFN_HIDDEN = 2816
CHUNK = 128
EPS = 1e-6

P_Z = 0
P_XBC = P_Z + SSD_WIDTH
P_GQ = P_XBC + SSD_CONV_DIM
P_GK = P_GQ + GLA_QK
P_GV = P_GK + GLA_QK
P_GG = P_GV + GLA_WIDTH
P_RQ = P_GG + GLA_WIDTH
P_RK = P_RQ + RET_WIDTH
P_RV = P_RK + RET_WIDTH
P_RG = P_RV + RET_WIDTH
P_SM = P_RG + RET_WIDTH
LANES = 128
PROJ_W = P_SM + LANES
SM_DT = 0
SM_GLR = SSD_HEADS

FFN_CK = 256
VMEM_LIMIT = 56 * 1024 * 1024

_NT = (((1,), (1,)), ((), ()))
_TN = (((0,), (0,)), ((), ()))


def _cparams(*sem):
    return pltpu.CompilerParams(dimension_semantics=sem, vmem_limit_bytes=VMEM_LIMIT)


def _const_spec(shape):
    nd = len(shape)
    return pl.BlockSpec(shape, lambda *_: (0,) * nd, pipeline_mode=pl.Buffered(1))


def _rms(x, w):
    return x * lax.rsqrt(jnp.mean(x * x, axis=-1, keepdims=True) + EPS) * w


def _bdot(a, b):
    return jnp.dot(a.astype(BF16), b.astype(BF16), preferred_element_type=F32)


def _bdot_nt(a, b):
    return lax.dot_general(a.astype(BF16), b.astype(BF16), _NT, preferred_element_type=F32)


def _bdot_tn(a, b):
    return lax.dot_general(a.astype(BF16), b.astype(BF16), _TN, preferred_element_type=F32)


def _ffn_kernel(x_ref, nw_ref, wup_ref, wdn_ref, o_ref, acc_ref):
    x = x_ref[...]
    h = _rms(x, nw_ref[...]).astype(BF16)
    for c in range(FFN_HIDDEN // FFN_CK):
        lo = c * FFN_CK
        g = jnp.dot(h, wup_ref[:, lo:lo + FFN_CK], preferred_element_type=F32)
        u = jnp.dot(h, wup_ref[:, FFN_HIDDEN + lo:FFN_HIDDEN + lo + FFN_CK], preferred_element_type=F32)
        a = (jax.nn.silu(g) * u).astype(BF16)
        d = jnp.dot(a, wdn_ref[lo:lo + FFN_CK, :], preferred_element_type=F32)
        if c == 0:
            acc_ref[...] = d
        else:
            acc_ref[...] += d
    o_ref[...] = x + 0.5 * acc_ref[...]


def _ffn(x, nw, wup, wdn, tm):
    m = x.shape[0]
    return pl.pallas_call(
        _ffn_kernel,
        grid=(m // tm,),
        in_specs=[pl.BlockSpec((tm, D_MODEL), lambda i: (i, 0)), _const_spec(nw.shape), _const_spec(wup.shape),
                  _const_spec(wdn.shape)],
        out_specs=pl.BlockSpec((tm, D_MODEL), lambda i: (i, 0)),
        out_shape=jax.ShapeDtypeStruct((m, D_MODEL), F32),
        scratch_shapes=[pltpu.VMEM((tm, D_MODEL), F32)],
        compiler_params=_cparams("parallel"),
        name="ffn",
    )(x, nw, wup, wdn)


def _norm_matmul_kernel(x_ref, nw_ref, w_ref, o_ref):
    h = _rms(x_ref[...], nw_ref[...]).astype(BF16)
    o_ref[...] = jnp.dot(h, w_ref[...], preferred_element_type=F32)


def _norm_matmul(x, nw, w, tm):
    m, n = x.shape[0], w.shape[1]
    return pl.pallas_call(
        _norm_matmul_kernel,
        grid=(m // tm,),
        in_specs=[pl.BlockSpec((tm, D_MODEL), lambda i: (i, 0)), _const_spec(nw.shape), _const_spec(w.shape)],
        out_specs=pl.BlockSpec((tm, n), lambda i: (i, 0)),
        out_shape=jax.ShapeDtypeStruct((m, n), F32),
        compiler_params=_cparams("parallel"),
        name="norm_matmul",
    )(x, nw, w)


def _matmul_kernel(a_ref, w_ref, o_ref):
    o_ref[...] = _bdot(a_ref[...], w_ref[...])


def _matmul(a, w, tm):
    m, k = a.shape
    n = w.shape[1]
    return pl.pallas_call(
        _matmul_kernel,
        grid=(m // tm,),
        in_specs=[pl.BlockSpec((tm, k), lambda i: (i, 0)), _const_spec(w.shape)],
        out_specs=pl.BlockSpec((tm, n), lambda i: (i, 0)),
        out_shape=jax.ShapeDtypeStruct((m, n), F32),
        compiler_params=_cparams("parallel"),
        name="matmul",
    )(a, w)


def _matmul_res_kernel(a_ref, w_ref, x_ref, o_ref):
    o_ref[...] = x_ref[...] + _bdot(a_ref[...], w_ref[...])


def _matmul_res(a, w, x, tm):
    m, k = a.shape
    return pl.pallas_call(
        _matmul_res_kernel,
        grid=(m // tm,),
        in_specs=[pl.BlockSpec((tm, k), lambda i: (i, 0)), _const_spec(w.shape),
                  pl.BlockSpec((tm, D_MODEL), lambda i: (i, 0))],
        out_specs=pl.BlockSpec((tm, D_MODEL), lambda i: (i, 0)),
        out_shape=jax.ShapeDtypeStruct((m, D_MODEL), F32),
        compiler_params=_cparams("parallel"),
        name="matmul_res",
    )(a, w, x)


def _rmsnorm_kernel(x_ref, nw_ref, o_ref):
    o_ref[...] = _rms(x_ref[...], nw_ref[...])


def _rmsnorm(x, nw, tm):
    m = x.shape[0]
    return pl.pallas_call(
        _rmsnorm_kernel,
        grid=(m // tm,),
        in_specs=[pl.BlockSpec((tm, D_MODEL), lambda i: (i, 0)), _const_spec(nw.shape)],
        out_specs=pl.BlockSpec((tm, D_MODEL), lambda i: (i, 0)),
        out_shape=jax.ShapeDtypeStruct((m, D_MODEL), F32),
        compiler_params=_cparams("parallel"),
        name="rmsnorm",
    )(x, nw)


def _softmax_pv(s, v):
    s = s * XATTN_HEAD_DIM ** -0.5
    p = jnp.exp(s - jnp.max(s, axis=-1, keepdims=True))
    return _bdot(p, v) / jnp.sum(p, axis=-1, keepdims=True)


def _attn_prompt_kernel(q_ref, k_ref, v_ref, o_ref):
    for h in range(XATTN_HEADS):
        sl = slice(h * XATTN_HEAD_DIM, (h + 1) * XATTN_HEAD_DIM)
        s = _bdot_nt(q_ref[:, sl], k_ref[:, sl])
        o_ref[:, sl] = _softmax_pv(s, v_ref[:, sl])


def _attn_prompt(q, mk, mv, nb, seq, tq):
    nq = seq // tq
    return pl.pallas_call(
        _attn_prompt_kernel,
        grid=(nb, nq),
        in_specs=[pl.BlockSpec((tq, D_MODEL), lambda b, i: (b * nq + i, 0)),
                  pl.BlockSpec((MEM_LEN, D_MODEL), lambda b, i: (b, 0)),
                  pl.BlockSpec((MEM_LEN, D_MODEL), lambda b, i: (b, 0))],
        out_specs=pl.BlockSpec((tq, D_MODEL), lambda b, i: (b * nq + i, 0)),
        out_shape=jax.ShapeDtypeStruct(q.shape, F32),
        compiler_params=_cparams("parallel", "parallel"),
        name="attn_prompt",
    )(q, mk, mv)


def _attn_sample_kernel(q_ref, k_ref, v_ref, o_ref, *, bs):
    for j in range(bs):
        for h in range(XATTN_HEADS):
            sl = slice(h * XATTN_HEAD_DIM, (h + 1) * XATTN_HEAD_DIM)
            q8 = jnp.broadcast_to(q_ref[j:j + 1, sl], (8, XATTN_HEAD_DIM))
            o8 = _softmax_pv(_bdot_nt(q8, k_ref[j, :, sl]), v_ref[j, :, sl])
            o_ref[j:j + 1, sl] = o8[0:1, :]


def _attn_sample(q, ck, cv, bs):
    n = q.shape[0]
    return pl.pallas_call(
        functools.partial(_attn_sample_kernel, bs=bs),
        grid=(n // bs,),
        in_specs=[pl.BlockSpec((bs, D_MODEL), lambda i: (i, 0)),
                  pl.BlockSpec((bs, MEM_LEN, D_MODEL), lambda i: (i, 0, 0)),
                  pl.BlockSpec((bs, MEM_LEN, D_MODEL), lambda i: (i, 0, 0))],
        out_specs=pl.BlockSpec((bs, D_MODEL), lambda i: (i, 0)),
        out_shape=jax.ShapeDtypeStruct(q.shape, F32),
        compiler_params=_cparams("parallel"),
        name="attn_sample",
    )(q, ck, cv)


def _swap_halves(x):
    w = x.shape[-1]
    lane = lax.broadcasted_iota(jnp.int32, x.shape, x.ndim - 1)
    half = HEAD_DIM // 2
    return jnp.where(lane % HEAD_DIM < half, pltpu.roll(x, w - half, x.ndim - 1), pltpu.roll(x, half, x.ndim - 1))


def _rotary(x, cos, sin_signed):
    return x * cos + _swap_halves(x) * sin_signed


def _col(row):
    return jnp.broadcast_to(row, (8, row.shape[-1])).T[:, 0:1]


def _gla_intra(q, k, b, c):
    t_i = lax.broadcasted_iota(jnp.int32, (c, c), 0)
    s_i = lax.broadcasted_iota(jnp.int32, (c, c), 1)
    row = lax.broadcasted_iota(jnp.int32, (c, 1), 0)
    acc = [jnp.zeros((c, c), F32) for _ in range(GLA_HEADS)]
    size = c // 2
    while size >= 16:
        span = 2 * size
        ref_rows = jnp.concatenate(
            [jnp.broadcast_to(b[u + size - 1:u + size, :], (span, GLA_QK)) for u in range(0, c, span)], axis=0)
        odd = (row // size) % 2 == 1
        qt = jnp.where(odd, q * jnp.exp(jnp.minimum(b - ref_rows, 0.0)), 0.0).astype(BF16)
        kt = jnp.where(odd, 0.0, k * jnp.exp(jnp.minimum(ref_rows - b, 0.0))).astype(BF16)
        same = (t_i // span) == (s_i // span)
        for h in range(GLA_HEADS):
            sl = slice(h * GLA_DK, (h + 1) * GLA_DK)
            sc = lax.dot_general(qt[:, sl], kt[:, sl], _NT, preferred_element_type=F32)
            acc[h] = acc[h] + jnp.where(same, sc, 0.0)
        size //= 2
    blk = 16
    for j in range(blk):
        kj = jnp.concatenate([jnp.broadcast_to(k[u + j:u + j + 1, :], (blk, GLA_QK)) for u in range(0, c, blk)], axis=0)
        bj = jnp.concatenate([jnp.broadcast_to(b[u + j:u + j + 1, :], (blk, GLA_QK)) for u in range(0, c, blk)], axis=0)
        p = jnp.where(row % blk >= j, q * kj * jnp.exp(jnp.minimum(b - bj, 0.0)), 0.0)
        hit = s_i == (t_i // blk) * blk + j
        for h in range(GLA_HEADS):
            sc = jnp.sum(p[:, h * GLA_DK:(h + 1) * GLA_DK], axis=-1, keepdims=True)
            acc[h] = acc[h] + jnp.where(hit, sc, 0.0)
    return acc


def _small_params(sm, dtb, alog, wg, bg):
    dt = jax.nn.softplus(sm + dtb)
    la_ssd = dt * (-jnp.exp(alog))
    la_gla = jax.nn.log_sigmoid(_bdot(sm, wg) + bg) / GLA_TAU
    return dt, la_ssd, la_gla


def _ssd_gate_norm(y, z, w):
    g = y * jax.nn.silu(z)
    gw = SSD_WIDTH // SSD_GROUPS
    parts = []
    for gi in range(SSD_GROUPS):
        gg = g[:, gi * gw:(gi + 1) * gw]
        parts.append(gg * lax.rsqrt(jnp.mean(gg * gg, axis=-1, keepdims=True) + EPS))
    return jnp.concatenate(parts, axis=-1) * w


def _head_rms(o):
    return o * lax.rsqrt(jnp.mean(o * o, axis=-1, keepdims=True) + EPS)


def _head_gn(o):
    d = o - jnp.mean(o, axis=-1, keepdims=True)
    return d * lax.rsqrt(jnp.mean(d * d, axis=-1, keepdims=True) + EPS)


def _core_prompt_kernel(proj_ref, cw_ref, cb_ref, dtb_ref, alog_ref, dsk_ref, ssdw_ref, wg_ref, bg_ref, glaw_ref,
                        retw_ref, lg_ref, cos_ref, sin_ref,
                        y_ref, ssd_o, conv_o, gla_o, ret_o,
                        ext_ref, s_ssd, s_gla, s_ret, ysc, *, c):
    ci = pl.program_id(1)
    last = pl.num_programs(1) - 1

    @pl.when(ci == 0)
    def _():
        ext_ref[0:8, :] = jnp.zeros((8, SSD_CONV_DIM), F32)
        s_ssd[...] = jnp.zeros(s_ssd.shape, F32)
        s_gla[...] = jnp.zeros(s_gla.shape, F32)
        s_ret[...] = jnp.zeros(s_ret.shape, F32)

    t_i = lax.broadcasted_iota(jnp.int32, (c, c), 0)
    s_i = lax.broadcasted_iota(jnp.int32, (c, c), 1)
    tri = s_i <= t_i
    tri_f = tri.astype(F32)
    row = lax.broadcasted_iota(jnp.int32, (c, 1), 0)

    ext_ref[8:8 + c, :] = proj_ref[:, P_XBC:P_XBC + SSD_CONV_DIM]
    conv = cb_ref[...]
    for j in range(SSD_CONV):
        conv = conv + cw_ref[j:j + 1, :] * ext_ref[pl.ds(8 - (SSD_CONV - 1) + j, c), :]
    keep = ext_ref[c + 5:c + 8, :]
    ext_ref[5:8, :] = keep
    xbc = jax.nn.silu(conv)
    xs = xbc[:, :SSD_WIDTH]

    dt, la_ssd, la_gla = _small_params(proj_ref[:, P_SM:P_SM + LANES], dtb_ref[...], alog_ref[...], wg_ref[...],
                                       bg_ref[...])

    b = jnp.dot(tri_f, la_ssd, precision=lax.Precision.HIGHEST, preferred_element_type=F32)
    b_t = b.T
    dt_t = dt.T
    for g in range(SSD_GROUPS):
        bm = xbc[:, SSD_WIDTH + g * SSD_STATE:SSD_WIDTH + (g + 1) * SSD_STATE]
        cm = xbc[:, SSD_WIDTH + (SSD_GROUPS + g) * SSD_STATE:SSD_WIDTH + (SSD_GROUPS + g + 1) * SSD_STATE]
        gmat = _bdot_nt(cm, bm)
        for h in range(g * SSD_HEADS // SSD_GROUPS, (g + 1) * SSD_HEADS // SSD_GROUPS):
            sl = slice(h * HEAD_DIM, (h + 1) * HEAD_DIM)
            bcol = b[:, h:h + 1]
            blast = b[c - 1:c, h:h + 1]
            decay = jnp.exp(jnp.where(tri, bcol - b_t[h:h + 1, :], -jnp.inf))
            scores = gmat * decay * dt_t[h:h + 1, :]
            xh = xs[:, sl]
            s_old = s_ssd[h]
            o = _bdot(scores, xh) + _bdot(cm, s_old) * jnp.exp(bcol)
            wcol = dt[:, h:h + 1] * jnp.exp(blast - bcol)
            s_ssd[h] = s_old * jnp.exp(blast) + _bdot_tn(bm * wcol, xh)
            ysc[:, sl] = o + xh * dsk_ref[:, sl]
    y_ref[:, :SSD_WIDTH] = _ssd_gate_norm(ysc[:, :SSD_WIDTH], proj_ref[:, P_Z:P_Z + SSD_WIDTH],
                                          ssdw_ref[...]).astype(y_ref.dtype)

    gq = proj_ref[:, P_GQ:P_GQ + GLA_QK] * GLA_DK ** -0.5
    gk = proj_ref[:, P_GK:P_GK + GLA_QK]
    gv = proj_ref[:, P_GV:P_GV + GLA_WIDTH]
    bg_ = jnp.dot(tri_f, la_gla, precision=lax.Precision.HIGHEST, preferred_element_type=F32)
    a_intra = _gla_intra(gq, gk, bg_, c)
    q_in = gq * jnp.exp(bg_)
    bl_row = bg_[c - 1:c, :]
    k_out = gk * jnp.exp(bl_row - bg_)
    upd = _bdot_tn(k_out, gv)
    ebl_col = _col(jnp.exp(bl_row))
    for h in range(GLA_HEADS):
        ks = slice(h * GLA_DK, (h + 1) * GLA_DK)
        vs = slice(h * HEAD_DIM, (h + 1) * HEAD_DIM)
        s_old = s_gla[ks, :]
        o = _bdot(a_intra[h], gv[:, vs]) + _bdot(q_in[:, ks], s_old)
        s_gla[ks, :] = s_old * ebl_col[ks, :] + upd[ks, vs]
        ysc[:, SSD_WIDTH + h * HEAD_DIM:SSD_WIDTH + (h + 1) * HEAD_DIM] = _head_rms(o)
    lo = SSD_WIDTH
    y_ref[:, lo:lo + GLA_WIDTH] = (ysc[:, lo:lo + GLA_WIDTH] * glaw_ref[...] *
                                   jax.nn.silu(proj_ref[:, P_GG:P_GG + GLA_WIDTH])).astype(y_ref.dtype)

    cos = cos_ref[...]
    sin = sin_ref[...]
    rq = _rotary(proj_ref[:, P_RQ:P_RQ + RET_WIDTH], cos, sin)
    rk = _rotary(proj_ref[:, P_RK:P_RK + RET_WIDTH], cos, sin) * HEAD_DIM ** -0.5
    rv = proj_ref[:, P_RV:P_RV + RET_WIDTH]
    dts = (t_i - s_i).astype(F32)
    for h in range(RET_HEADS):
        sl = slice(h * HEAD_DIM, (h + 1) * HEAD_DIM)
        lg = lg_ref[:, h * HEAD_DIM:h * HEAD_DIM + 1]
        decay = jnp.exp(jnp.where(tri, dts * lg, -jnp.inf))
        qh, kh, vh = rq[:, sl], rk[:, sl], rv[:, sl]
        s_old = s_ret[sl, :]
        o = _bdot(_bdot_nt(qh, kh) * decay, vh) + _bdot(qh, s_old) * jnp.exp((row + 1).astype(F32) * lg)
        tail = jnp.exp((c - 1 - row).astype(F32) * lg)
        s_ret[sl, :] = s_old * jnp.exp(c * lg) + _bdot_tn(kh * tail, vh)
        ysc[:, lo + GLA_WIDTH + h * HEAD_DIM:lo + GLA_WIDTH + (h + 1) * HEAD_DIM] = _head_gn(o)
    lo2 = lo + GLA_WIDTH
    y_ref[:, lo2:lo2 + RET_WIDTH] = (ysc[:, lo2:lo2 + RET_WIDTH] * retw_ref[...] *
                                     jax.nn.silu(proj_ref[:, P_RG:P_RG + RET_WIDTH])).astype(y_ref.dtype)

    @pl.when(ci == last)
    def _():
        ssd_o[0] = s_ssd[...]
        conv_o[0] = keep
        gla_o[0] = s_gla[...]
        ret_o[0] = s_ret[...]


def _core_prompt(proj, mp, cos, sin, nb, seq):
    c = math.gcd(seq, CHUNK)
    nc = seq // c
    consts = [mp["conv_w"], mp["conv_b"], mp["dt_bias"], mp["a_log"], mp["d_skip"], mp["ssd_norm_w"], mp["wg"],
              mp["bg"], mp["gla_norm_w"], mp["ret_norm_w"], mp["lg"]]
    return pl.pallas_call(
        functools.partial(_core_prompt_kernel, c=c),
        grid=(nb, nc),
        in_specs=[pl.BlockSpec((c, PROJ_W), lambda b, i: (b * nc + i, 0))] + [_const_spec(a.shape) for a in consts] +
                 [pl.BlockSpec((c, RET_WIDTH), lambda b, i: (i, 0)), pl.BlockSpec((c, RET_WIDTH), lambda b, i: (i, 0))],
        out_specs=[pl.BlockSpec((c, D_MODEL), lambda b, i: (b * nc + i, 0)),
                   pl.BlockSpec((1, SSD_HEADS, SSD_STATE, HEAD_DIM), lambda b, i: (b, 0, 0, 0)),
                   pl.BlockSpec((1, SSD_CONV - 1, SSD_CONV_DIM), lambda b, i: (b, 0, 0)),
                   pl.BlockSpec((1, GLA_QK, HEAD_DIM), lambda b, i: (b, 0, 0)),
                   pl.BlockSpec((1, RET_WIDTH, HEAD_DIM), lambda b, i: (b, 0, 0))],
        out_shape=[jax.ShapeDtypeStruct((nb * seq, D_MODEL), BF16),
                   jax.ShapeDtypeStruct((nb, SSD_HEADS, SSD_STATE, HEAD_DIM), F32),
                   jax.ShapeDtypeStruct((nb, SSD_CONV - 1, SSD_CONV_DIM), F32),
                   jax.ShapeDtypeStruct((nb, GLA_QK, HEAD_DIM), F32),
                   jax.ShapeDtypeStruct((nb, RET_WIDTH, HEAD_DIM), F32)],
        scratch_shapes=[pltpu.VMEM((c + 8, SSD_CONV_DIM), F32), pltpu.VMEM((SSD_HEADS, SSD_STATE, HEAD_DIM), F32),
                        pltpu.VMEM((GLA_QK, HEAD_DIM), F32), pltpu.VMEM((RET_WIDTH, HEAD_DIM), F32),
                        pltpu.VMEM((c, D_MODEL), F32)],
        compiler_params=_cparams("parallel", "arbitrary"),
        name="core_prompt",
    )(proj, *consts, cos, sin)


def _core_sample_kernel(proj_ref, cst_ref, ssd_ref, gla_ref, ret_ref, cw_ref, cb_ref, dtb_ref, alog_ref, dsk_ref,
                        ssdw_ref, wg_ref, bg_ref, glaw_ref, retw_ref, lg_ref, cos_ref, sin_ref,
                        y_ref, conv_o, ssd_o, gla_o, ret_o, ysc, *, bs):
    cd = SSD_CONV_DIM
    u = proj_ref[:, P_XBC:P_XBC + cd]
    conv = cb_ref[...] + cw_ref[SSD_CONV - 1:SSD_CONV, :] * u
    for j in range(SSD_CONV - 1):
        conv = conv + cw_ref[j:j + 1, :] * cst_ref[:, j * cd:(j + 1) * cd]
    conv_o[:, :(SSD_CONV - 2) * cd] = cst_ref[:, cd:]
    conv_o[:, (SSD_CONV - 2) * cd:] = u
    xbc = jax.nn.silu(conv)
    xs = xbc[:, :SSD_WIDTH]

    dt, la_ssd, la_gla = _small_params(proj_ref[:, P_SM:P_SM + LANES], dtb_ref[...], alog_ref[...], wg_ref[...],
                                       bg_ref[...])

    a_ssd = jnp.exp(la_ssd)
    for g in range(SSD_GROUPS):
        bm_t = xbc[:, SSD_WIDTH + g * SSD_STATE:SSD_WIDTH + (g + 1) * SSD_STATE].T
        cm_t = xbc[:, SSD_WIDTH + (SSD_GROUPS + g) * SSD_STATE:SSD_WIDTH + (SSD_GROUPS + g + 1) * SSD_STATE].T
        for h in range(g * SSD_HEADS // SSD_GROUPS, (g + 1) * SSD_HEADS // SSD_GROUPS):
            sl = slice(h * HEAD_DIM, (h + 1) * HEAD_DIM)
            for j in range(bs):
                xrow = xs[j:j + 1, sl]
                s_new = ssd_ref[j, h] * a_ssd[j:j + 1, h:h + 1] + (bm_t[:, j:j + 1] * dt[j:j + 1, h:h + 1]) * xrow
                ssd_o[j, h] = s_new
                o = jnp.sum(cm_t[:, j:j + 1] * s_new, axis=0, keepdims=True)
                ysc[j:j + 1, sl] = o + xrow * dsk_ref[:, sl]
    y_ref[:, :SSD_WIDTH] = _ssd_gate_norm(ysc[:, :SSD_WIDTH], proj_ref[:, P_Z:P_Z + SSD_WIDTH],
                                          ssdw_ref[...]).astype(y_ref.dtype)

    gq_t = (proj_ref[:, P_GQ:P_GQ + GLA_QK] * GLA_DK ** -0.5).T
    gk_t = proj_ref[:, P_GK:P_GK + GLA_QK].T
    ea_t = jnp.exp(la_gla).T
    gv = proj_ref[:, P_GV:P_GV + GLA_WIDTH]
    lo = SSD_WIDTH
    for j in range(bs):
        vrows = jnp.concatenate([jnp.broadcast_to(gv[j:j + 1, h * HEAD_DIM:(h + 1) * HEAD_DIM], (GLA_DK, HEAD_DIM))
                                 for h in range(GLA_HEADS)], axis=0)
        s_new = gla_ref[j] * ea_t[:, j:j + 1] + gk_t[:, j:j + 1] * vrows
        gla_o[j] = s_new
        qs = gq_t[:, j:j + 1] * s_new
        for h in range(GLA_HEADS):
            o = jnp.sum(qs[h * GLA_DK:(h + 1) * GLA_DK, :], axis=0, keepdims=True)
            ysc[j:j + 1, lo + h * HEAD_DIM:lo + (h + 1) * HEAD_DIM] = _head_rms(o)
    y_ref[:, lo:lo + GLA_WIDTH] = (ysc[:, lo:lo + GLA_WIDTH] * glaw_ref[...] *
                                   jax.nn.silu(proj_ref[:, P_GG:P_GG + GLA_WIDTH])).astype(y_ref.dtype)

    cos = cos_ref[...]
    sin = sin_ref[...]
    rq_t = _rotary(proj_ref[:, P_RQ:P_RQ + RET_WIDTH], cos, sin).T
    rk_t = (_rotary(proj_ref[:, P_RK:P_RK + RET_WIDTH], cos, sin) * HEAD_DIM ** -0.5).T
    rv = proj_ref[:, P_RV:P_RV + RET_WIDTH]
    gam_col = _col(jnp.exp(lg_ref[...]))
    lo2 = lo + GLA_WIDTH
    for j in range(bs):
        vrows = jnp.concatenate([jnp.broadcast_to(rv[j:j + 1, h * HEAD_DIM:(h + 1) * HEAD_DIM], (HEAD_DIM, HEAD_DIM))
                                 for h in range(RET_HEADS)], axis=0)
        s_new = ret_ref[j] * gam_col + rk_t[:, j:j + 1] * vrows
        ret_o[j] = s_new
        qs = rq_t[:, j:j + 1] * s_new
        for h in range(RET_HEADS):
            o = jnp.sum(qs[h * HEAD_DIM:(h + 1) * HEAD_DIM, :], axis=0, keepdims=True)
            ysc[j:j + 1, lo2 + h * HEAD_DIM:lo2 + (h + 1) * HEAD_DIM] = _head_gn(o)
    y_ref[:, lo2:lo2 + RET_WIDTH] = (ysc[:, lo2:lo2 + RET_WIDTH] * retw_ref[...] *
                                     jax.nn.silu(proj_ref[:, P_RG:P_RG + RET_WIDTH])).astype(y_ref.dtype)


def _core_sample(proj, conv_st, ssd_st, gla_st, ret_st, mp, cos, sin, bs):
    n = proj.shape[0]
    consts = [mp["conv_w"], mp["conv_b"], mp["dt_bias"], mp["a_log"], mp["d_skip"], mp["ssd_norm_w"], mp["wg"],
              mp["bg"], mp["gla_norm_w"], mp["ret_norm_w"], mp["lg"], cos, sin]
    cst_w = (SSD_CONV - 1) * SSD_CONV_DIM
    return pl.pallas_call(
        functools.partial(_core_sample_kernel, bs=bs),
        grid=(n // bs,),
        in_specs=[pl.BlockSpec((bs, PROJ_W), lambda i: (i, 0)),
                  pl.BlockSpec((bs, cst_w), lambda i: (i, 0)),
                  pl.BlockSpec((bs, SSD_HEADS, SSD_STATE, HEAD_DIM), lambda i: (i, 0, 0, 0)),
                  pl.BlockSpec((bs, GLA_QK, HEAD_DIM), lambda i: (i, 0, 0)),
                  pl.BlockSpec((bs, RET_WIDTH, HEAD_DIM), lambda i: (i, 0, 0))] + [_const_spec(a.shape) for a in consts],
        out_specs=[pl.BlockSpec((bs, D_MODEL), lambda i: (i, 0)),
                   pl.BlockSpec((bs, cst_w), lambda i: (i, 0)),
                   pl.BlockSpec((bs, SSD_HEADS, SSD_STATE, HEAD_DIM), lambda i: (i, 0, 0, 0)),
                   pl.BlockSpec((bs, GLA_QK, HEAD_DIM), lambda i: (i, 0, 0)),
                   pl.BlockSpec((bs, RET_WIDTH, HEAD_DIM), lambda i: (i, 0, 0))],
        out_shape=[jax.ShapeDtypeStruct((n, D_MODEL), BF16),
                   jax.ShapeDtypeStruct((n, cst_w), F32),
                   jax.ShapeDtypeStruct(ssd_st.shape, F32),
                   jax.ShapeDtypeStruct(gla_st.shape, F32),
                   jax.ShapeDtypeStruct(ret_st.shape, F32)],
        scratch_shapes=[pltpu.VMEM((bs, D_MODEL), F32)],
        compiler_params=_cparams("parallel"),
        name="core_sample",
    )(proj, conv_st, ssd_st, gla_st, ret_st, *consts)


def _pack_w_in(w):
    cut1 = SSD_WIDTH + SSD_CONV_DIM
    dt = w[:, cut1:cut1 + SSD_HEADS]
    a0 = cut1 + SSD_HEADS
    mid = w[:, a0:a0 + 2 * GLA_QK + 2 * GLA_WIDTH]
    a1 = a0 + 2 * GLA_QK + 2 * GLA_WIDTH
    glr = w[:, a1:a1 + GLA_RANK]
    ret = w[:, a1 + GLA_RANK:]
    pad = jnp.zeros((w.shape[0], LANES - SSD_HEADS - GLA_RANK), w.dtype)
    return jnp.concatenate([w[:, :cut1], mid, ret, dt, glr, pad], axis=1).astype(BF16)


def _row(v, width=None):
    v = v.reshape(1, -1).astype(F32)
    if width is not None and v.shape[1] < width:
        v = jnp.pad(v, ((0, 0), (0, width - v.shape[1])))
    return v


def _rope_tables(pos):
    half = HEAD_DIM // 2
    inv = ROPE_BASE ** (-jnp.arange(half, dtype=F32) / half)
    ang = pos.astype(F32)[:, None] * inv[None, :]
    cos = jnp.cos(ang)
    sin = jnp.sin(ang)
    cos = jnp.tile(jnp.concatenate([cos, cos], axis=-1), (1, RET_HEADS))
    sin = jnp.tile(jnp.concatenate([-sin, sin], axis=-1), (1, RET_HEADS))
    return cos, sin


def kernel(x_prompt, x_sample, mem_prompt, cache_mem_k, cache_mem_v, state_ssd, state_ssd_conv, state_gla, state_ret,
           norm_ffn1, ffn1_w_up, ffn1_w_down, norm_mix, w_in, conv_w, conv_b, dt_bias, a_log, d_skip, ssd_norm_w,
           gla_w_gate, gla_b_gate, gla_norm_w, ret_norm_w, w_out, norm_xattn, xattn_w_q, xattn_w_k, xattn_w_v,
           xattn_w_o, norm_ffn2, ffn2_w_up, ffn2_w_down, final_norm_w):
    nb, seq, _ = x_prompt.shape
    ns = x_sample.shape[0]
    tm_p = math.gcd(nb * seq, 512)
    tm_s = ns
    tq = math.gcd(seq, 512)
    bs_core = math.gcd(ns, 8)
    bs_attn = math.gcd(ns, 8)

    xp = x_prompt.reshape(nb * seq, D_MODEL)
    xs = x_sample.reshape(ns, D_MODEL)
    mem = mem_prompt.reshape(nb * MEM_LEN, D_MODEL)
    cos_p, sin_p = _rope_tables(jnp.arange(seq, dtype=jnp.int32))
    cos_s, sin_s = _rope_tables(PAST_LEN + jnp.arange(1, dtype=jnp.int32))
    log_gamma = jnp.log1p(-jnp.exp2(-5.0 - jnp.arange(RET_HEADS, dtype=F32)))
    lg_row = jnp.repeat(log_gamma, HEAD_DIM).reshape(1, RET_WIDTH)

    outs = {k: [] for k in ("ssd_p", "conv_p", "gla_p", "ret_p", "mk_p", "mv_p", "ssd_s", "conv_s", "gla_s", "ret_s")}
    for l in range(DEPTH):
        f1u, f1d = ffn1_w_up[l].astype(BF16), ffn1_w_down[l].astype(BF16)
        f2u, f2d = ffn2_w_up[l].astype(BF16), ffn2_w_down[l].astype(BF16)
        win = _pack_w_in(w_in[l])
        wout = w_out[l].astype(BF16)
        wq, wk, wv, wo = (w[l].astype(BF16) for w in (xattn_w_q, xattn_w_k, xattn_w_v, xattn_w_o))
        wg = jnp.zeros((LANES, GLA_QK), F32).at[SM_GLR:SM_GLR + GLA_RANK].set(gla_w_gate[l]).astype(BF16)
        mp = dict(conv_w=conv_w[l], conv_b=_row(conv_b[l]), dt_bias=_row(dt_bias[l], LANES), a_log=_row(a_log[l], LANES),
                  d_skip=_row(jnp.repeat(d_skip[l], HEAD_DIM)), ssd_norm_w=_row(ssd_norm_w[l]), wg=wg,
                  bg=_row(gla_b_gate[l]), gla_norm_w=_row(gla_norm_w[l]), ret_norm_w=_row(ret_norm_w[l]), lg=lg_row)
        n1, n2, n3, n4 = (_row(w[l]) for w in (norm_ffn1, norm_mix, norm_xattn, norm_ffn2))

        mk = _matmul(mem, wk, math.gcd(nb * MEM_LEN, 512))
        mv = _matmul(mem, wv, math.gcd(nb * MEM_LEN, 512))

        xp = _ffn(xp, n1, f1u, f1d, tm_p)
        proj = _norm_matmul(xp, n2, win, tm_p)
        y, s1, c1, g1, r1 = _core_prompt(proj, mp, cos_p, sin_p, nb, seq)
        xp = _matmul_res(y, wout, xp, tm_p)
        q = _norm_matmul(xp, n3, wq, tm_p)
        o = _attn_prompt(q, mk, mv, nb, seq, tq)
        xp = _matmul_res(o, wo, xp, tm_p)
        xp = _ffn(xp, n4, f2u, f2d, tm_p)

        xs = _ffn(xs, n1, f1u, f1d, tm_s)
        proj = _norm_matmul(xs, n2, win, tm_s)
        y, c2, s2, g2, r2 = _core_sample(
            proj, state_ssd_conv[l].reshape(ns, -1), state_ssd[l], state_gla[l].reshape(ns, GLA_QK, HEAD_DIM),
            state_ret[l].reshape(ns, RET_WIDTH, HEAD_DIM), mp, cos_s, sin_s, bs_core)
        xs = _matmul_res(y, wout, xs, tm_s)
        q = _norm_matmul(xs, n3, wq, tm_s)
        o = _attn_sample(q, cache_mem_k[l].reshape(ns, MEM_LEN, D_MODEL), cache_mem_v[l].reshape(ns, MEM_LEN, D_MODEL),
                         bs_attn)
        xs = _matmul_res(o, wo, xs, tm_s)
        xs = _ffn(xs, n4, f2u, f2d, tm_s)

        outs["ssd_p"].append(s1)
        outs["conv_p"].append(c1)
        outs["gla_p"].append(g1.reshape(nb, GLA_HEADS, GLA_DK, HEAD_DIM))
        outs["ret_p"].append(r1.reshape(nb, RET_HEADS, HEAD_DIM, HEAD_DIM))
        outs["mk_p"].append(mk.reshape(nb, MEM_LEN, XATTN_HEADS, XATTN_HEAD_DIM))
        outs["mv_p"].append(mv.reshape(nb, MEM_LEN, XATTN_HEADS, XATTN_HEAD_DIM))
        outs["ssd_s"].append(s2)
        outs["conv_s"].append(c2.reshape(ns, SSD_CONV - 1, SSD_CONV_DIM))
        outs["gla_s"].append(g2.reshape(ns, GLA_HEADS, GLA_DK, HEAD_DIM))
        outs["ret_s"].append(r2.reshape(ns, RET_HEADS, HEAD_DIM, HEAD_DIM))

    fn = _row(final_norm_w)
    y_prompt = _rmsnorm(xp, fn, tm_p).reshape(nb, seq, D_MODEL)
    y_sample = _rmsnorm(xs, fn, tm_s).reshape(ns, 1, D_MODEL)
    st = {k: jnp.stack(v) for k, v in outs.items()}
    return (y_prompt, y_sample, st["ssd_p"], st["conv_p"], st["gla_p"], st["ret_p"], st["mk_p"], st["mv_p"],
            st["ssd_s"], st["conv_s"], st["gla_s"], st["ret_s"])
```

```python
import functools
import math

import numpy as np
import jax
import jax.numpy as jnp
from jax import lax
from jax.experimental import pallas as pl
from jax.experimental.pallas import tpu as pltpu

F32 = jnp.float32
BF16 = jnp.bfloat16

D_MODEL = 1024
DEPTH = 4
PAST_LEN = 16384
HEAD_DIM = 64
SSD_WIDTH = 512
SSD_HEADS = 8
SSD_GROUPS = 2
SSD_STATE = 128
SSD_CONV = 4
SSD_CONV_DIM = SSD_WIDTH + 2 * SSD_GROUPS * SSD_STATE
GLA_WIDTH = 256
GLA_HEADS = 4
GLA_DK = 32
GLA_QK = GLA_HEADS * GLA_DK
GLA_RANK = 16
GLA_TAU = 16.0
RET_WIDTH = 256
RET_HEADS = 4
ROPE_BASE = 10000.0
MEM_LEN = 256
XATTN_HEADS = 4
XATTN_HEAD_DIM = D_MODEL // XATTN_HEADS
FFN_HIDDEN = 2816
CHUNK = 128
EPS = 1e-6

P_Z = 0
P_XBC = P_Z + SSD_WIDTH
P_GQ = P_XBC + SSD_CONV_DIM
P_GK = P_GQ + GLA_QK
P_GV = P_GK + GLA_QK
P_GG = P_GV + GLA_WIDTH
P_RQ = P_GG + GLA_WIDTH
P_RK = P_RQ + RET_WIDTH
P_RV = P_RK + RET_WIDTH
P_RG = P_RV + RET_WIDTH
P_SM = P_RG + RET_WIDTH
LANES = 128
SUBLANES = 8
PROJ_W = P_SM + LANES
SM_GLR = SSD_HEADS
GROUP_W = SSD_WIDTH // SSD_GROUPS
HPG = SSD_HEADS // SSD_GROUPS

FFN_CK = 256
GLA_DIAG = 8
VMEM_LIMIT = 56 * 1024 * 1024

_NT = (((1,), (1,)), ((), ()))
_TN = (((0,), (0,)), ((), ()))
_HI = lax.Precision.HIGHEST


def _cparams(*sem):
    return pltpu.CompilerParams(dimension_semantics=sem, vmem_limit_bytes=VMEM_LIMIT)


def _const_spec(shape):
    nd = len(shape)
    return pl.BlockSpec(shape, lambda *_: (0,) * nd, pipeline_mode=pl.Buffered(1))


def _rms(x, w):
    return x * lax.rsqrt(jnp.mean(x * x, axis=-1, keepdims=True) + EPS) * w


def _bdot(a, b):
    return jnp.dot(a.astype(BF16), b.astype(BF16), preferred_element_type=F32)


def _bdot_nt(a, b):
    return lax.dot_general(a.astype(BF16), b.astype(BF16), _NT, preferred_element_type=F32)


def _bdot_tn(a, b):
    return lax.dot_general(a.astype(BF16), b.astype(BF16), _TN, preferred_element_type=F32)


def _col(row):
    return jnp.broadcast_to(row, (SUBLANES, row.shape[-1])).T[:, 0:1]


def _softmax_pv(s, v):
    s = s * XATTN_HEAD_DIM ** -0.5
    p = jnp.exp(s - jnp.max(s, axis=-1, keepdims=True))
    return _bdot(p, v) / jnp.sum(p, axis=-1, keepdims=True)


def _tok_kernel(*refs, outproj, attn, emit_q, oproj, ffn, inproj, final):
    it = iter(refs)
    x_ref = next(it)
    if outproj:
        y_ref, wout_ref = next(it), next(it)
    if attn or emit_q:
        n3_ref, wq_ref = next(it), next(it)
    if attn:
        k_ref, v_ref = next(it), next(it)
    if oproj:
        oin_ref = next(it)
    if attn or oproj:
        wo_ref = next(it)
    if ffn:
        nf_ref, wup_ref, wdn_ref = next(it), next(it), next(it)
    if inproj:
        n2_ref, win_ref = next(it), next(it)
    if final:
        fw_ref = next(it)
    xo_ref = next(it)
    if emit_q:
        q_ref = next(it)
    if inproj:
        proj_ref = next(it)
    if ffn:
        acc_ref = next(it)
    if attn:
        o_scr = next(it)

    x = x_ref[...]
    if outproj:
        x = x + jnp.dot(y_ref[...], wout_ref[...], preferred_element_type=F32)
    if attn or emit_q:
        q = jnp.dot(_rms(x, n3_ref[...]).astype(BF16), wq_ref[...], preferred_element_type=F32)
        if emit_q:
            q_ref[...] = q
    if attn:
        for h in range(XATTN_HEADS):
            sl = slice(h * XATTN_HEAD_DIM, (h + 1) * XATTN_HEAD_DIM)
            o_scr[:, sl] = _softmax_pv(_bdot_nt(q[:, sl], k_ref[:, sl]), v_ref[:, sl]).astype(BF16)
        x = x + jnp.dot(o_scr[...], wo_ref[...], preferred_element_type=F32)
    if oproj:
        x = x + _bdot(oin_ref[...], wo_ref[...])
    xo_ref[...] = x
    if ffn:
        h = _rms(xo_ref[...], nf_ref[...]).astype(BF16)
        for c in range(FFN_HIDDEN // FFN_CK):
            lo = c * FFN_CK
            g = jnp.dot(h, wup_ref[:, lo:lo + FFN_CK], preferred_element_type=F32)
            u = jnp.dot(h, wup_ref[:, FFN_HIDDEN + lo:FFN_HIDDEN + lo + FFN_CK], preferred_element_type=F32)
            d = jnp.dot((jax.nn.silu(g) * u).astype(BF16), wdn_ref[lo:lo + FFN_CK, :], preferred_element_type=F32)
            if c == 0:
                acc_ref[...] = d
            else:
                acc_ref[...] += d
        xo_ref[...] = xo_ref[...] + 0.5 * acc_ref[...]
    if inproj:
        proj_ref[...] = jnp.dot(_rms(xo_ref[...], n2_ref[...]).astype(BF16), win_ref[...], preferred_element_type=F32)
    if final:
        xo_ref[...] = _rms(xo_ref[...], fw_ref[...])


def _tok_call(x, nb, tm, *, y=None, wout=None, n3=None, wq=None, kv=None, o_in=None, wo=None, nf=None, wup=None,
              wdn=None, n2=None, win=None, fw=None, emit_q=False, name="tok"):
    m = x.shape[0]
    nt = m // nb // tm
    tile = lambda w: pl.BlockSpec((tm, w), lambda b, i: (b * nt + i, 0))
    flags = dict(outproj=y is not None, attn=kv is not None, emit_q=emit_q, oproj=o_in is not None, ffn=nf is not None,
                 inproj=win is not None, final=fw is not None)
    args, specs = [x], [tile(D_MODEL)]

    def add_const(*ws):
        for w in ws:
            args.append(w)
            specs.append(_const_spec(w.shape))

    if flags["outproj"]:
        args.append(y)
        specs.append(tile(D_MODEL))
        add_const(wout)
    if flags["attn"] or emit_q:
        add_const(n3, wq)
    if flags["attn"]:
        args += [kv, kv]
        specs += [pl.BlockSpec((MEM_LEN, D_MODEL), lambda b, i: (b, 0)),
                  pl.BlockSpec((MEM_LEN, D_MODEL), lambda b, i: (b, 1))]
    if flags["oproj"]:
        args.append(o_in)
        specs.append(tile(D_MODEL))
    if flags["attn"] or flags["oproj"]:
        add_const(wo)
    if flags["ffn"]:
        add_const(nf, wup, wdn)
    if flags["inproj"]:
        add_const(n2, win)
    if flags["final"]:
        add_const(fw)
    out_shape, out_specs = [jax.ShapeDtypeStruct((m, D_MODEL), F32)], [tile(D_MODEL)]
    if emit_q:
        out_shape.append(jax.ShapeDtypeStruct((m, D_MODEL), F32))
        out_specs.append(tile(D_MODEL))
    if flags["inproj"]:
        out_shape.append(jax.ShapeDtypeStruct((m, PROJ_W), F32))
        out_specs.append(tile(PROJ_W))
    scratch = []
    if flags["ffn"]:
        scratch.append(pltpu.VMEM((tm, D_MODEL), F32))
    if flags["attn"]:
        scratch.append(pltpu.VMEM((tm, D_MODEL), BF16))
    return pl.pallas_call(
        functools.partial(_tok_kernel, **flags),
        grid=(nb, nt), in_specs=specs, out_specs=out_specs, out_shape=out_shape, scratch_shapes=scratch,
        compiler_params=_cparams("parallel", "parallel"), name=name,
    )(*args)


def _memkv_kernel(a_ref, w_ref, o_ref, o16_ref):
    r = _bdot(a_ref[...], w_ref[...])
    o_ref[...] = r
    o16_ref[...] = r.astype(BF16)


def _memkv(mem, wkv, tm):
    m, n = mem.shape[0], wkv.shape[1]
    return pl.pallas_call(
        _memkv_kernel,
        grid=(m // tm,),
        in_specs=[pl.BlockSpec((tm, D_MODEL), lambda i: (i, 0)), _const_spec(wkv.shape)],
        out_specs=[pl.BlockSpec((tm, n), lambda i: (i, 0)), pl.BlockSpec((tm, n), lambda i: (i, 0))],
        out_shape=[jax.ShapeDtypeStruct((m, n), F32), jax.ShapeDtypeStruct((m, n), BF16)],
        compiler_params=_cparams("parallel"), name="memkv",
    )(mem, wkv)


KV_ROWS = MEM_LEN * XATTN_HEADS * (XATTN_HEAD_DIM // LANES)
KV_RPM = KV_ROWS // MEM_LEN


def _kv_rows_view(c):
    l, ns = c.shape[:2]
    c = c.reshape(l, ns, MEM_LEN, XATTN_HEADS, XATTN_HEAD_DIM // LANES, LANES)
    return jnp.transpose(c, (0, 1, 2, 4, 3, 5)).reshape(l, ns, KV_ROWS, LANES)


def _attn_sample_kernel(q_ref, k_ref, v_ref, o_ref, *, bs):
    lane = lax.broadcasted_iota(jnp.int32, (KV_RPM, KV_ROWS), 1)
    sub = lax.broadcasted_iota(jnp.int32, (KV_RPM, KV_ROWS), 0)
    valid = (lane % KV_RPM) == sub
    first_half = lane[0:1] % KV_RPM < XATTN_HEADS
    halves = XATTN_HEAD_DIM // LANES
    pieces = [(d, h, slice(h * XATTN_HEAD_DIM + d * LANES, h * XATTN_HEAD_DIM + (d + 1) * LANES))
              for d in range(halves) for h in range(XATTN_HEADS)]
    for j in range(bs):
        q8 = jnp.concatenate([q_ref[j:j + 1, sl] for _, _, sl in pieces], axis=0)
        s_all = _bdot_nt(q8, k_ref[j])
        u = jnp.sum(jnp.where(valid, s_all, 0.0), axis=0, keepdims=True)
        u = u + jnp.where(first_half, pltpu.roll(u, KV_ROWS - XATTN_HEADS, 1), pltpu.roll(u, XATTN_HEADS, 1))
        s = jnp.where(valid, u * XATTN_HEAD_DIM ** -0.5, -jnp.inf)
        p = jnp.exp(s - jnp.max(s, axis=1, keepdims=True))
        o = _bdot(p, v_ref[j]) / jnp.sum(p, axis=1, keepdims=True)
        for d, h, sl in pieces:
            o_ref[j:j + 1, sl] = o[d * XATTN_HEADS + h:d * XATTN_HEADS + h + 1, :]


def _attn_sample(q, ck, cv, layer, bs):
    n = q.shape[0]
    kv_spec = pl.BlockSpec((None, bs, KV_ROWS, LANES), lambda i: (layer, i, 0, 0))
    return pl.pallas_call(
        functools.partial(_attn_sample_kernel, bs=bs),
        grid=(n // bs,),
        in_specs=[pl.BlockSpec((bs, D_MODEL), lambda i: (i, 0)), kv_spec, kv_spec],
        out_specs=pl.BlockSpec((bs, D_MODEL), lambda i: (i, 0)),
        out_shape=jax.ShapeDtypeStruct(q.shape, F32),
        compiler_params=_cparams("parallel"), name="attn_sample",
    )(q, ck, cv)


def _swap_halves(x):
    w = x.shape[-1]
    lane = lax.broadcasted_iota(jnp.int32, x.shape, x.ndim - 1)
    half = HEAD_DIM // 2
    return jnp.where(lane % HEAD_DIM < half, pltpu.roll(x, w - half, x.ndim - 1), pltpu.roll(x, half, x.ndim - 1))


def _rotary(x, cos, sin_signed):
    return x * cos + _swap_halves(x) * sin_signed


def _rows_of_block(x, j, blk):
    c, w = x.shape
    x3 = x.reshape(c // blk, blk, w)
    return jnp.broadcast_to(x3[:, j:j + 1, :], (c // blk, blk, w)).reshape(c, w)


def _small_params(sm, dtb, wg, bg):
    dt = jax.nn.softplus(sm + dtb)
    la_gla = jax.nn.log_sigmoid(_bdot(sm, wg) + bg) / GLA_TAU
    return dt, la_gla


def _group_rms(g):
    return jnp.concatenate(
        [g[:, i * GROUP_W:(i + 1) * GROUP_W] *
         lax.rsqrt(jnp.mean(jnp.square(g[:, i * GROUP_W:(i + 1) * GROUP_W]), axis=-1, keepdims=True) + EPS)
         for i in range(SSD_GROUPS)], axis=-1)


def _np_consts(c):
    t = np.arange(c)
    tri = (t[None, :] <= t[:, None]).astype(np.float32)
    expand = np.zeros((LANES, SSD_WIDTH), np.float32)
    for h in range(SSD_HEADS):
        expand[h, h * HEAD_DIM:(h + 1) * HEAD_DIM] = 1.0
    hq = np.zeros((SUBLANES, GLA_QK), np.float32)
    hv = np.zeros((SUBLANES, GLA_WIDTH), np.float32)
    for h in range(GLA_HEADS):
        hq[h, h * GLA_DK:(h + 1) * GLA_DK] = 1.0
        hv[h, h * HEAD_DIM:(h + 1) * HEAD_DIM] = 1.0
    bd_gla = (np.arange(GLA_QK)[:, None] // GLA_DK == np.arange(GLA_WIDTH)[None, :] // HEAD_DIM).astype(np.float32)
    bd_ret = (np.arange(RET_WIDTH)[:, None] // HEAD_DIM == np.arange(RET_WIDTH)[None, :] // HEAD_DIM).astype(np.float32)
    return dict(tri=jnp.asarray(tri), expand=jnp.asarray(expand), hq=jnp.asarray(hq), hv=jnp.asarray(hv),
                bd_gla=jnp.asarray(bd_gla), bd_ret=jnp.asarray(bd_ret))


def _core_prompt_kernel(proj_ref, cw_ref, cb_ref, dtb_ref, aexp_ref, dsk_ref, ssdw_ref, wg_ref, bg_ref, glaw_ref,
                        retw_ref, lg_ref, tri_ref, exp_ref, hq_ref, hv_ref, bdg_ref, bdr_ref, cos_ref, sin_ref,
                        y_ref, ssd_o, conv_o, gla_o, ret_o,
                        ext_ref, s_ssd, s_gla, s_ret, *, c):
    bi = pl.program_id(0)
    ci = pl.program_id(1)
    last = pl.num_programs(1) - 1

    @pl.when(ci == 0)
    def _():
        ext_ref[0:SUBLANES, :] = jnp.zeros((SUBLANES, SSD_CONV_DIM), F32)
        s_ssd[...] = jnp.zeros(s_ssd.shape, F32)
        s_gla[...] = jnp.zeros(s_gla.shape, F32)
        s_ret[...] = jnp.zeros(s_ret.shape, F32)

    t_i = lax.broadcasted_iota(jnp.int32, (c, c), 0)
    s_i = lax.broadcasted_iota(jnp.int32, (c, c), 1)
    tri = s_i <= t_i
    tri_f = tri_ref[...]
    row = lax.broadcasted_iota(jnp.int32, (c, 1), 0)
    hv = hv_ref[...]

    ext_ref[SUBLANES:SUBLANES + c, :] = proj_ref[:, P_XBC:P_XBC + SSD_CONV_DIM]
    conv = cb_ref[...]
    for j in range(SSD_CONV):
        conv = conv + cw_ref[j:j + 1, :] * ext_ref[pl.ds(SUBLANES - (SSD_CONV - 1) + j, c), :]
    keep = ext_ref[c + SUBLANES - (SSD_CONV - 1):c + SUBLANES, :]
    ext_ref[SUBLANES - (SSD_CONV - 1):SUBLANES, :] = keep
    xbc = jax.nn.silu(conv)

    dt, la_gla = _small_params(proj_ref[:, P_SM:P_SM + LANES], dtb_ref[...], wg_ref[...], bg_ref[...])

    aexp = aexp_ref[...]
    dt_e = jnp.dot(dt, exp_ref[...], precision=_HI, preferred_element_type=F32)
    b_e = jnp.dot(tri_f, dt_e * aexp, precision=_HI, preferred_element_type=F32)
    dt_t = dt.T
    y_parts = []
    for g in range(SSD_GROUPS):
        gl = slice(g * GROUP_W, (g + 1) * GROUP_W)
        bm = xbc[:, SSD_WIDTH + g * SSD_STATE:SSD_WIDTH + (g + 1) * SSD_STATE]
        cm = xbc[:, SSD_WIDTH + (SSD_GROUPS + g) * SSD_STATE:SSD_WIDTH + (SSD_GROUPS + g + 1) * SSD_STATE]
        xs_g = xbc[:, gl]
        xs16 = xs_g.astype(BF16)
        gmat = _bdot_nt(cm, bm)
        b_g = b_e[:, gl]
        b_gt = b_g.T
        o_g = jnp.zeros((c, GROUP_W), F32)
        for hh in range(HPG):
            h = g * HPG + hh
            decay = jnp.exp(jnp.where(tri, b_g[:, hh * HEAD_DIM:hh * HEAD_DIM + 1] -
                                      b_gt[hh * HEAD_DIM:hh * HEAD_DIM + 1, :], -jnp.inf))
            scores = (gmat * decay * dt_t[h:h + 1, :]).astype(BF16)
            o_g = o_g + hv[hh:hh + 1, :] * jnp.dot(scores, xs16, preferred_element_type=F32)
        blast = b_g[c - 1:c, :]
        s_old = s_ssd[g]
        o_g = o_g + _bdot_nt(cm, s_old) * jnp.exp(b_g)
        w_e = dt_e[:, gl] * jnp.exp(blast - b_g)
        s_ssd[g] = s_old * _col(jnp.exp(blast)) + _bdot_tn(xs_g * w_e, bm)
        y_parts.append(o_g + xs_g * dsk_ref[:, gl])
    y_ssd = jnp.concatenate(y_parts, axis=-1) * jax.nn.silu(proj_ref[:, P_Z:P_Z + SSD_WIDTH])
    y_ref[:, :SSD_WIDTH] = (_group_rms(y_ssd) * ssdw_ref[...]).astype(y_ref.dtype)

    hq = hq_ref[...]
    bdg = bdg_ref[...]
    gq = proj_ref[:, P_GQ:P_GQ + GLA_QK] * GLA_DK ** -0.5
    gk = proj_ref[:, P_GK:P_GK + GLA_QK]
    gv = proj_ref[:, P_GV:P_GV + GLA_WIDTH]
    gv16 = gv.astype(BF16)
    b = jnp.dot(tri_f, la_gla, precision=_HI, preferred_element_type=F32)
    acc = [None] * GLA_HEADS
    size = c // 2
    while size >= GLA_DIAG:
        span = 2 * size
        ref_rows = jnp.concatenate(
            [jnp.broadcast_to(b[u + size - 1:u + size, :], (span, GLA_QK)) for u in range(0, c, span)], axis=0)
        odd = (row // size) % 2 == 1
        qt = jnp.where(odd, gq * jnp.exp(jnp.minimum(b - ref_rows, 0.0)), 0.0)
        kt = jnp.where(odd, 0.0, gk * jnp.exp(jnp.minimum(ref_rows - b, 0.0))).astype(BF16)
        same = (t_i // span) == (s_i // span)
        for h in range(GLA_HEADS):
            sc = lax.dot_general((qt * hq[h:h + 1, :]).astype(BF16), kt, _NT, preferred_element_type=F32)
            if span < c:
                sc = jnp.where(same, sc, 0.0)
            acc[h] = sc if acc[h] is None else acc[h] + sc
        size //= 2
    o_gla = _bdot(gq * jnp.exp(b), s_gla[...])
    for h in range(GLA_HEADS):
        o_gla = o_gla + hv[h:h + 1, :] * jnp.dot(acc[h].astype(BF16), gv16, preferred_element_type=F32)
    bdg16 = bdg.astype(BF16)
    for j in range(GLA_DIAG):
        p = jnp.where(row % GLA_DIAG >= j,
                      gq * _rows_of_block(gk, j, GLA_DIAG) * jnp.exp(jnp.minimum(b - _rows_of_block(b, j, GLA_DIAG), 0.0)),
                      0.0)
        o_gla = o_gla + jnp.dot(p.astype(BF16), bdg16, preferred_element_type=F32) * _rows_of_block(gv, j, GLA_DIAG)
    bl_row = b[c - 1:c, :]
    s_gla[...] = s_gla[...] * _col(jnp.exp(bl_row)) + bdg * _bdot_tn(gk * jnp.exp(bl_row - b), gv)
    avg = bdr_ref[...] * (1.0 / HEAD_DIM)
    ms = jnp.dot(o_gla * o_gla, avg, precision=_HI, preferred_element_type=F32)
    lo = SSD_WIDTH
    y_ref[:, lo:lo + GLA_WIDTH] = (o_gla * lax.rsqrt(ms + EPS) * glaw_ref[...] *
                                   jax.nn.silu(proj_ref[:, P_GG:P_GG + GLA_WIDTH])).astype(y_ref.dtype)

    cos = cos_ref[...]
    sin = sin_ref[...]
    lg = lg_ref[...]
    rq = _rotary(proj_ref[:, P_RQ:P_RQ + RET_WIDTH], cos, sin)
    rk = _rotary(proj_ref[:, P_RK:P_RK + RET_WIDTH], cos, sin) * HEAD_DIM ** -0.5
    rv = proj_ref[:, P_RV:P_RV + RET_WIDTH]
    rk16 = rk.astype(BF16)
    rv16 = rv.astype(BF16)
    dts = (t_i - s_i).astype(F32)
    o_ret = _bdot(rq, s_ret[...]) * jnp.exp((row + 1).astype(F32) * lg)
    for h in range(RET_HEADS):
        decay = jnp.exp(jnp.where(tri, dts * lg[:, h * HEAD_DIM:h * HEAD_DIM + 1], -jnp.inf))
        sc = lax.dot_general((rq * hv[h:h + 1, :]).astype(BF16), rk16, _NT, preferred_element_type=F32) * decay
        o_ret = o_ret + hv[h:h + 1, :] * jnp.dot(sc.astype(BF16), rv16, preferred_element_type=F32)
    tail = jnp.exp((c - 1 - row).astype(F32) * lg)
    s_ret[...] = s_ret[...] * jnp.exp(c * lg) + bdr_ref[...] * _bdot_tn(rk * tail, rv)
    mu = jnp.dot(o_ret, avg, precision=_HI, preferred_element_type=F32)
    d = o_ret - mu
    var = jnp.dot(d * d, avg, precision=_HI, preferred_element_type=F32)
    lo2 = lo + GLA_WIDTH
    y_ref[:, lo2:lo2 + RET_WIDTH] = (d * lax.rsqrt(var + EPS) * retw_ref[...] *
                                     jax.nn.silu(proj_ref[:, P_RG:P_RG + RET_WIDTH])).astype(y_ref.dtype)

    @pl.when(ci == last)
    def _():
        ssd_o[0] = s_ssd[...].reshape(SSD_HEADS, HEAD_DIM, SSD_STATE)
        for j in range(SSD_CONV - 1):
            conv_o[j, pl.ds(bi, 1), :] = keep[j:j + 1, :]
        sg = s_gla[...]
        sr = s_ret[...]
        for h in range(GLA_HEADS):
            gla_o[0, h * GLA_DK:(h + 1) * GLA_DK, :] = sg[h * GLA_DK:(h + 1) * GLA_DK, h * HEAD_DIM:(h + 1) * HEAD_DIM]
        for h in range(RET_HEADS):
            ret_o[0, h * HEAD_DIM:(h + 1) * HEAD_DIM, :] = sr[h * HEAD_DIM:(h + 1) * HEAD_DIM,
                                                              h * HEAD_DIM:(h + 1) * HEAD_DIM]


def _core_prompt(proj, mp, kc, cos, sin, nb, seq):
    c = math.gcd(seq, CHUNK)
    nc = seq // c
    consts = [mp["conv_w"], mp["conv_b"], mp["dt_bias"], mp["a_exp"], mp["d_skip"], mp["ssd_norm_w"], mp["wg"],
              mp["bg"], mp["gla_norm_w"], mp["ret_norm_w"], mp["lg"], kc["tri"], kc["expand"], kc["hq"], kc["hv"],
              kc["bd_gla"], kc["bd_ret"]]
    return pl.pallas_call(
        functools.partial(_core_prompt_kernel, c=c),
        grid=(nb, nc),
        in_specs=[pl.BlockSpec((c, PROJ_W), lambda b, i: (b * nc + i, 0))] + [_const_spec(a.shape) for a in consts] +
                 [pl.BlockSpec((c, RET_WIDTH), lambda b, i: (i, 0)), pl.BlockSpec((c, RET_WIDTH), lambda b, i: (i, 0))],
        out_specs=[pl.BlockSpec((c, D_MODEL), lambda b, i: (b * nc + i, 0)),
                   pl.BlockSpec((1, SSD_HEADS, HEAD_DIM, SSD_STATE), lambda b, i: (b, 0, 0, 0)),
                   pl.BlockSpec((SSD_CONV - 1, nb, SSD_CONV_DIM), lambda b, i: (0, 0, 0)),
                   pl.BlockSpec((1, GLA_QK, HEAD_DIM), lambda b, i: (b, 0, 0)),
                   pl.BlockSpec((1, RET_WIDTH, HEAD_DIM), lambda b, i: (b, 0, 0))],
        out_shape=[jax.ShapeDtypeStruct((nb * seq, D_MODEL), BF16),
                   jax.ShapeDtypeStruct((nb, SSD_HEADS, HEAD_DIM, SSD_STATE), F32),
                   jax.ShapeDtypeStruct((SSD_CONV - 1, nb, SSD_CONV_DIM), F32),
                   jax.ShapeDtypeStruct((nb, GLA_QK, HEAD_DIM), F32),
                   jax.ShapeDtypeStruct((nb, RET_WIDTH, HEAD_DIM), F32)],
        scratch_shapes=[pltpu.VMEM((c + SUBLANES, SSD_CONV_DIM), F32),
                        pltpu.VMEM((SSD_GROUPS, GROUP_W, SSD_STATE), F32),
                        pltpu.VMEM((GLA_QK, GLA_WIDTH), F32), pltpu.VMEM((RET_WIDTH, RET_WIDTH), F32)],
        compiler_params=_cparams("arbitrary", "arbitrary"), name="core_prompt",
    )(proj, *consts, cos, sin)


def _sample_ssd_kernel(*refs, bs, chained):
    if chained:
        refs = refs[2:]
    (proj_ref, cst_ref, ssd_ref, cw_ref, cb_ref, dtb_ref, alog_ref, dsk_ref, ssdw_ref,
     y_ref, conv_o, ssd_o, ot_ref) = refs
    cd = SSD_CONV_DIM
    u = proj_ref[:, P_XBC:P_XBC + cd]
    conv = cb_ref[...] + cw_ref[SSD_CONV - 1:SSD_CONV, :] * u
    for j in range(SSD_CONV - 1):
        conv = conv + cw_ref[j:j + 1, :] * cst_ref[j]
        conv_o[j] = cst_ref[j + 1] if j + 1 < SSD_CONV - 1 else u
    xbc = jax.nn.silu(conv)
    xs = xbc[:, :SSD_WIDTH]
    xs_t = xs.T
    dt = jax.nn.softplus(proj_ref[:, P_SM:P_SM + LANES] + dtb_ref[...])
    a = jnp.exp(dt * (-jnp.exp(alog_ref[...])))
    ot_ref[...] = jnp.zeros(ot_ref.shape, F32)
    for h in range(SSD_HEADS):
        g = h // HPG
        bm = xbc[:, SSD_WIDTH + g * SSD_STATE:SSD_WIDTH + (g + 1) * SSD_STATE]
        cm = xbc[:, SSD_WIDTH + (SSD_GROUPS + g) * SSD_STATE:SSD_WIDTH + (SSD_GROUPS + g + 1) * SSD_STATE]
        pr = slice(h * HEAD_DIM, (h + 1) * HEAD_DIM)
        for j in range(bs):
            s_new = (ssd_ref[j, h] * a[j:j + 1, h:h + 1] +
                     xs_t[pr, j:j + 1] * (bm[j:j + 1, :] * dt[j:j + 1, h:h + 1]))
            ssd_o[j, h] = s_new
            ot_ref[pr, j:j + 1] = jnp.sum(s_new * cm[j:j + 1, :], axis=1, keepdims=True)
    o = ot_ref[...].T[0:bs, :]
    y = (o + xs * dsk_ref[...]) * jax.nn.silu(proj_ref[:, P_Z:P_Z + SSD_WIDTH])
    y_ref[...] = (_group_rms(y) * ssdw_ref[...]).astype(y_ref.dtype)


def _sample_ssd(proj, conv_all, ssd_all, prev, layer, mp, bs):
    ns = proj.shape[0]
    consts = [mp["conv_w"], mp["conv_b"], mp["dt_bias"], mp["a_log"], mp["d_skip"], mp["ssd_norm_w"]]
    chained = prev is not None
    conv_spec = pl.BlockSpec((None, SSD_CONV - 1, bs, SSD_CONV_DIM), lambda i: (layer, 0, i, 0))
    ssd_spec = pl.BlockSpec((None, bs, SSD_HEADS, HEAD_DIM, SSD_STATE), lambda i: (layer, i, 0, 0, 0))
    any_spec = pl.BlockSpec(memory_space=pl.ANY)
    return pl.pallas_call(
        functools.partial(_sample_ssd_kernel, bs=bs, chained=chained),
        grid=(ns // bs,),
        in_specs=([any_spec, any_spec] if chained else []) +
                 [pl.BlockSpec((bs, PROJ_W), lambda i: (i, 0)), conv_spec, ssd_spec] + [_const_spec(a.shape) for a in consts],
        out_specs=[pl.BlockSpec((bs, SSD_WIDTH), lambda i: (i, 0)), conv_spec, ssd_spec],
        out_shape=[jax.ShapeDtypeStruct((ns, SSD_WIDTH), BF16), jax.ShapeDtypeStruct(conv_all.shape, F32),
                   jax.ShapeDtypeStruct(ssd_all.shape, F32)],
        scratch_shapes=[pltpu.VMEM((SSD_WIDTH, LANES), F32)],
        input_output_aliases={0: 1, 1: 2} if chained else {},
        compiler_params=_cparams("parallel"), name="sample_ssd",
    )(*(list(prev) if chained else []), proj, conv_all, ssd_all, *consts)


def _sample_gr_kernel(proj_ref, gla_ref, ret_ref, wg_ref, bg_ref, glaw_ref, retw_ref, lg_ref, cos_ref, sin_ref,
                      y_ref, gla_o, ret_o):
    sm = proj_ref[:, P_SM:P_SM + LANES]
    la = jax.nn.log_sigmoid(_bdot(sm, wg_ref[...]) + bg_ref[...]) / GLA_TAU
    ea_t = jnp.exp(la).T
    q_t = (proj_ref[:, P_GQ:P_GQ + GLA_QK] * GLA_DK ** -0.5).T
    k_t = proj_ref[:, P_GK:P_GK + GLA_QK].T
    v_t = proj_ref[:, P_GV:P_GV + GLA_WIDTH].T
    outs = []
    for h in range(GLA_HEADS):
        vh = v_t[h * HEAD_DIM:(h + 1) * HEAD_DIM, :]
        o = jnp.zeros_like(vh)
        for k in range(GLA_DK):
            r = h * GLA_DK + k
            s_new = gla_ref[h, k] * ea_t[r:r + 1, :] + k_t[r:r + 1, :] * vh
            gla_o[h, k] = s_new
            o = o + q_t[r:r + 1, :] * s_new
        outs.append(o * lax.rsqrt(jnp.mean(o * o, axis=0, keepdims=True) + EPS))
    y_gla = jnp.concatenate(outs, axis=0).T
    y_ref[:, :GLA_WIDTH] = (y_gla * glaw_ref[...] * jax.nn.silu(proj_ref[:, P_GG:P_GG + GLA_WIDTH])).astype(y_ref.dtype)

    cos = cos_ref[...]
    sin = sin_ref[...]
    q_t = _rotary(proj_ref[:, P_RQ:P_RQ + RET_WIDTH], cos, sin).T
    k_t = (_rotary(proj_ref[:, P_RK:P_RK + RET_WIDTH], cos, sin) * HEAD_DIM ** -0.5).T
    v_t = proj_ref[:, P_RV:P_RV + RET_WIDTH].T
    outs = []
    for h in range(RET_HEADS):
        vh = v_t[h * HEAD_DIM:(h + 1) * HEAD_DIM, :]
        gam = jnp.exp(lg_ref[:, h * HEAD_DIM:h * HEAD_DIM + 1])
        o = jnp.zeros_like(vh)
        for k in range(HEAD_DIM):
            r = h * HEAD_DIM + k
            s_new = ret_ref[h, k] * gam + k_t[r:r + 1, :] * vh
            ret_o[h, k] = s_new
            o = o + q_t[r:r + 1, :] * s_new
        d = o - jnp.mean(o, axis=0, keepdims=True)
        outs.append(d * lax.rsqrt(jnp.mean(d * d, axis=0, keepdims=True) + EPS))
    y_ret = jnp.concatenate(outs, axis=0).T
    y_ref[:, GLA_WIDTH:] = (y_ret * retw_ref[...] * jax.nn.silu(proj_ref[:, P_RG:P_RG + RET_WIDTH])).astype(y_ref.dtype)


def _sample_gr(proj, gla_all, ret_all, layer, mp, cos, sin):
    ns = proj.shape[0]
    consts = [mp["wg"], mp["bg"], mp["gla_norm_w"], mp["ret_norm_w"], mp["lg"], cos, sin]
    gshape = (GLA_HEADS, GLA_DK, HEAD_DIM, ns)
    rshape = (RET_HEADS, HEAD_DIM, HEAD_DIM, ns)
    return pl.pallas_call(
        _sample_gr_kernel,
        grid=(1,),
        in_specs=[pl.BlockSpec((ns, PROJ_W), lambda i: (0, 0)),
                  pl.BlockSpec((None,) + gshape, lambda i: (layer, 0, 0, 0, 0)),
                  pl.BlockSpec((None,) + rshape, lambda i: (layer, 0, 0, 0, 0))] + [_const_spec(a.shape) for a in consts],
        out_specs=[pl.BlockSpec((ns, GLA_WIDTH + RET_WIDTH), lambda i: (0, 0)),
                   pl.BlockSpec(gshape, lambda i: (0, 0, 0, 0)), pl.BlockSpec(rshape, lambda i: (0, 0, 0, 0))],
        out_shape=[jax.ShapeDtypeStruct((ns, GLA_WIDTH + RET_WIDTH), BF16), jax.ShapeDtypeStruct(gshape, F32),
                   jax.ShapeDtypeStruct(rshape, F32)],
        compiler_params=_cparams("arbitrary"), name="sample_gr",
    )(proj, gla_all, ret_all, *consts)


def _pack_w_in(w):
    cut1 = SSD_WIDTH + SSD_CONV_DIM
    dt = w[:, cut1:cut1 + SSD_HEADS]
    a0 = cut1 + SSD_HEADS
    mid = w[:, a0:a0 + 2 * GLA_QK + 2 * GLA_WIDTH]
    a1 = a0 + 2 * GLA_QK + 2 * GLA_WIDTH
    glr = w[:, a1:a1 + GLA_RANK]
    ret = w[:, a1 + GLA_RANK:]
    pad = jnp.zeros((w.shape[0], LANES - SSD_HEADS - GLA_RANK), w.dtype)
    return jnp.concatenate([w[:, :cut1], mid, ret, dt, glr, pad], axis=1).astype(BF16)


def _row(v, width=None):
    v = v.reshape(1, -1).astype(F32)
    if width is not None and v.shape[1] < width:
        v = jnp.pad(v, ((0, 0), (0, width - v.shape[1])))
    return v


def _rope_tables(pos):
    half = HEAD_DIM // 2
    inv = ROPE_BASE ** (-jnp.arange(half, dtype=F32) / half)
    ang = pos.astype(F32)[:, None] * inv[None, :]
    cos = jnp.cos(ang)
    sin = jnp.sin(ang)
    cos = jnp.tile(jnp.concatenate([cos, cos], axis=-1), (1, RET_HEADS))
    sin = jnp.tile(jnp.concatenate([-sin, sin], axis=-1), (1, RET_HEADS))
    return cos, sin


def kernel(x_prompt, x_sample, mem_prompt, cache_mem_k, cache_mem_v, state_ssd, state_ssd_conv, state_gla, state_ret,
           norm_ffn1, ffn1_w_up, ffn1_w_down, norm_mix, w_in, conv_w, conv_b, dt_bias, a_log, d_skip, ssd_norm_w,
           gla_w_gate, gla_b_gate, gla_norm_w, ret_norm_w, w_out, norm_xattn, xattn_w_q, xattn_w_k, xattn_w_v,
           xattn_w_o, norm_ffn2, ffn2_w_up, ffn2_w_down, final_norm_w):
    nb, seq, _ = x_prompt.shape
    ns = x_sample.shape[0]
    tm_p = math.gcd(seq, 512)
    bs = math.gcd(ns, SUBLANES)

    xp = x_prompt.reshape(nb * seq, D_MODEL)
    xs = x_sample.reshape(ns, D_MODEL)
    mem = mem_prompt.reshape(nb * MEM_LEN, D_MODEL)
    cos_p, sin_p = _rope_tables(jnp.arange(seq, dtype=jnp.int32))
    cos_s, sin_s = _rope_tables(PAST_LEN + jnp.arange(1, dtype=jnp.int32))
    log_gamma = jnp.log1p(-jnp.exp2(-5.0 - jnp.arange(RET_HEADS, dtype=F32)))
    lg_row = jnp.repeat(log_gamma, HEAD_DIM).reshape(1, RET_WIDTH)
    kc = _np_consts(math.gcd(seq, CHUNK))
    ck_rows, cv_rows = _kv_rows_view(cache_mem_k), _kv_rows_view(cache_mem_v)
    fw = _row(final_norm_w)

    conv_all = jnp.swapaxes(state_ssd_conv, 1, 2)
    ssd_all = jnp.swapaxes(state_ssd, 3, 4)
    gla_all = jnp.transpose(state_gla, (0, 2, 3, 4, 1))
    ret_all = jnp.transpose(state_ret, (0, 2, 3, 4, 1))

    outs = {k: [] for k in ("ssd_p", "conv_p", "gla_p", "ret_p", "mk_p", "mv_p", "gla_s", "ret_s")}
    chain = None
    for l in range(DEPTH):
        f1u, f1d = ffn1_w_up[l].astype(BF16), ffn1_w_down[l].astype(BF16)
        f2u, f2d = ffn2_w_up[l].astype(BF16), ffn2_w_down[l].astype(BF16)
        win = _pack_w_in(w_in[l])
        wout = w_out[l].astype(BF16)
        wq, wo = xattn_w_q[l].astype(BF16), xattn_w_o[l].astype(BF16)
        wkv = jnp.concatenate([xattn_w_k[l], xattn_w_v[l]], axis=1).astype(BF16)
        wg = jnp.zeros((LANES, GLA_QK), F32).at[SM_GLR:SM_GLR + GLA_RANK].set(gla_w_gate[l]).astype(BF16)
        mp = dict(conv_w=conv_w[l], conv_b=_row(conv_b[l]), dt_bias=_row(dt_bias[l], LANES), a_log=_row(a_log[l], LANES),
                  a_exp=_row(jnp.repeat(-jnp.exp(a_log[l]), HEAD_DIM)), d_skip=_row(jnp.repeat(d_skip[l], HEAD_DIM)),
                  ssd_norm_w=_row(ssd_norm_w[l]), wg=wg, bg=_row(gla_b_gate[l]), gla_norm_w=_row(gla_norm_w[l]),
                  ret_norm_w=_row(ret_norm_w[l]), lg=lg_row)
        n1, n2, n3, n4 = (_row(w[l]) for w in (norm_ffn1, norm_mix, norm_xattn, norm_ffn2))
        fin = fw if l == DEPTH - 1 else None

        kv, kv16 = _memkv(mem, wkv, math.gcd(nb * MEM_LEN, 512))

        xp, proj = _tok_call(xp, nb, tm_p, nf=n1, wup=f1u, wdn=f1d, n2=n2, win=win, name="ffn_inproj")
        y, s1, c1, g1, r1 = _core_prompt(proj, mp, kc, cos_p, sin_p, nb, seq)
        xp, = _tok_call(xp, nb, tm_p, y=y, wout=wout, n3=n3, wq=wq, kv=kv16, wo=wo, nf=n4, wup=f2u, wdn=f2d, fw=fin,
                        name="post_prompt")

        xs, proj = _tok_call(xs, 1, ns, nf=n1, wup=f1u, wdn=f1d, n2=n2, win=win, name="ffn_inproj_s")
        y_ssd, conv_chain, ssd_chain = _sample_ssd(proj, conv_all, ssd_all, chain, l, mp, bs)
        chain = (conv_chain, ssd_chain)
        y_gr, g2, r2 = _sample_gr(proj, gla_all, ret_all, l, mp, cos_s, sin_s)
        y = jnp.concatenate([y_ssd, y_gr], axis=1)
        xs, q = _tok_call(xs, 1, ns, y=y, wout=wout, n3=n3, wq=wq, emit_q=True, name="outq_s")
        o = _attn_sample(q, ck_rows, cv_rows, l, bs)
        xs, = _tok_call(xs, 1, ns, o_in=o, wo=wo, nf=n4, wup=f2u, wdn=f2d, fw=fin, name="oproj_ffn_s")

        outs["ssd_p"].append(s1)
        outs["conv_p"].append(c1)
        outs["gla_p"].append(g1.reshape(nb, GLA_HEADS, GLA_DK, HEAD_DIM))
        outs["ret_p"].append(r1.reshape(nb, RET_HEADS, HEAD_DIM, HEAD_DIM))
        outs["mk_p"].append(kv[:, :D_MODEL].reshape(nb, MEM_LEN, XATTN_HEADS, XATTN_HEAD_DIM))
        outs["mv_p"].append(kv[:, D_MODEL:].reshape(nb, MEM_LEN, XATTN_HEADS, XATTN_HEAD_DIM))
        outs["gla_s"].append(g2)
        outs["ret_s"].append(r2)

    st = {k: jnp.stack(v) for k, v in outs.items()}
    return (xp.reshape(nb, seq, D_MODEL), xs.reshape(ns, 1, D_MODEL),
            jnp.swapaxes(st["ssd_p"], 3, 4), jnp.swapaxes(st["conv_p"], 1, 2), st["gla_p"], st["ret_p"],
            st["mk_p"], st["mv_p"],
            jnp.swapaxes(chain[1], 3, 4), jnp.swapaxes(chain[0], 1, 2),
            jnp.transpose(st["gla_s"], (0, 4, 1, 2, 3)), jnp.transpose(st["ret_s"], (0, 4, 1, 2, 3)))
```
